```python
import jax, jax.numpy as jnp
from jax import lax
import numpy as np

D_MODEL = 1024
BATCH = 8
SEQ = 2048
DEPTH = 2
DEC_BATCH = 32
DEC_SEQ = 64
PAST_LEN = 4096

CHUNK = 64
POOL_WINDOWS = (2, 4, 8, 16)
N_POOL_GROUPS = len(POOL_WINDOWS)
POOL_GW = D_MODEL // N_POOL_GROUPS
POOL_BUF = max(POOL_WINDOWS) - 1
WINDOW = 128
N_BACK = WINDOW // CHUNK
HEAD_DIM = 64
N_HEADS = D_MODEL // HEAD_DIM
N_KV_HEADS = 4
GROUP = N_HEADS // N_KV_HEADS
D_FF = 7 * D_MODEL // 2
N_EXPERTS = 8
TOP_K = 2
PLE_DIM = 256
EPS = 1e-6
N_A = DEPTH // 2
N_B = DEPTH - N_A
N_DENSE = (DEPTH + 1) // 2
N_MOE = DEPTH // 2
NEG = -1e30

kernel_name = 'yoco_pool_swa_sink_stream_step'


def _rmsnorm(x, g):
    xf = x.astype(jnp.float32)
    y = xf * lax.rsqrt(jnp.mean(xf * xf, axis=-1, keepdims=True) + EPS)
    return (y * g.astype(jnp.float32)).astype(x.dtype)


def _kv_buf_len():
    return min(N_BACK * CHUNK, PAST_LEN)


def _pool_mixer(h, prev, mask, w_pool, scale):
    b, t, _ = h.shape
    p = prev.shape[1]
    cat = jnp.concatenate([prev.astype(h.dtype), h], axis=1).astype(jnp.float32)
    cs = jnp.pad(jnp.cumsum(cat, axis=1), ((0, 0), (1, 0), (0, 0)))
    cm = jnp.pad(jnp.cumsum(mask.astype(jnp.float32)), (1, 0))
    end = cs[:, p + 1:p + 1 + t]
    cnt_end = cm[p + 1:p + 1 + t]
    means = []
    for gi, w in enumerate(POOL_WINDOWS):
        sl = slice(gi * POOL_GW, (gi + 1) * POOL_GW)
        s = end[:, :, sl] - cs[:, p + 1 - w:p + 1 - w + t, sl]
        c = cnt_end - cm[p + 1 - w:p + 1 - w + t]
        means.append(s / c[None, :, None])
    diff = jnp.stack(means, axis=2) - h.astype(jnp.float32).reshape(b, t, N_POOL_GROUPS, POOL_GW)
    y = jnp.einsum('btgc,gce->btge', diff.astype(h.dtype), w_pool).reshape(b, t, D_MODEL)
    return y * scale


def _sink_softmax(s, sink):
    sink_b = jnp.broadcast_to(sink.astype(jnp.float32), s.shape[:-1] + (1,))
    pr = jax.nn.softmax(jnp.concatenate([s, sink_b], axis=-1), axis=-1)
    return pr[..., :-1]


def _swa_prompt(q, k, v, sink):
    b, s = q.shape[0], q.shape[1]
    nc = s // CHUNK
    pad = ((0, 0), (N_BACK * CHUNK, 0), (0, 0), (0, 0))
    kp = jnp.pad(k, pad).reshape(b, nc + N_BACK, CHUNK, N_KV_HEADS, HEAD_DIM)
    vp = jnp.pad(v, pad).reshape(b, nc + N_BACK, CHUNK, N_KV_HEADS, HEAD_DIM)
    kb = jnp.concatenate([kp[:, j:j + nc] for j in range(N_BACK + 1)], axis=2)
    vb = jnp.concatenate([vp[:, j:j + nc] for j in range(N_BACK + 1)], axis=2)
    key_chunk = jnp.arange(nc)[:, None] - N_BACK + jnp.arange(N_BACK + 1)[None, :]
    valid = jnp.repeat(key_chunk >= 0, CHUNK, axis=1)
    qc = q.reshape(b, nc, CHUNK, N_KV_HEADS, GROUP, HEAD_DIM)
    sc = jnp.einsum('bnqkgd,bnskd->bnkgqs', qc, kb,
                    preferred_element_type=jnp.float32) * (HEAD_DIM ** -0.5)
    sc = jnp.where(valid[None, :, None, None, None, :], sc, NEG)
    pr = _sink_softmax(sc, sink.reshape(N_KV_HEADS, GROUP)[None, None, :, :, None, None])
    o = jnp.einsum('bnkgqs,bnskd->bnqkgd', pr.astype(vb.dtype), vb)
    return o.reshape(b, s, N_HEADS * HEAD_DIM)


def _swa_sample(q, kk, vv, sink):
    b, t = q.shape[0], q.shape[1]
    sc = jnp.einsum('btkgd,bskd->bkgts', q, kk,
                    preferred_element_type=jnp.float32) * (HEAD_DIM ** -0.5)
    pr = _sink_softmax(sc, sink.reshape(N_KV_HEADS, GROUP)[None, :, :, None, None])
    o = jnp.einsum('bkgts,bskd->btkgd', pr.astype(vv.dtype), vv)
    return o.reshape(b, t, N_HEADS * HEAD_DIM)


def _swiglu(h, w1, w3, w2):
    return (jax.nn.silu(h @ w1) * (h @ w3)) @ w2


def _moe(h, w_router, b_router, w1, w3, w2):
    logits = jnp.einsum('btd,de->bte', h, w_router,
                        preferred_element_type=jnp.float32) + b_router.astype(jnp.float32)
    vals, idx = lax.top_k(logits, TOP_K)
    gates = jax.nn.softmax(vals, axis=-1)
    gate_full = jnp.sum(jax.nn.one_hot(idx, N_EXPERTS, dtype=jnp.float32) * gates[..., None], axis=-2)
    y = jnp.zeros_like(h)
    for e in range(N_EXPERTS):
        y = y + gate_full[..., e:e + 1].astype(h.dtype) * _swiglu(h, w1[e], w3[e], w2[e])
    return y


def _trunk(x, p, pool_prev, pool_mask, cache_k, cache_v,
           g_mix, g_ffn, g_ple, g_kv, g_final, w_pool, pool_scale,
           w_q, b_q, w_k, b_k, w_v, b_v, w_o, sinks,
           w1_dense, w3_dense, w2_dense, w_router, b_router, w1_moe, w3_moe, w2_moe,
           w_ple_proj, w_ple_gate):
    prompt = cache_k is None
    b, t, _ = x.shape
    kv_buf = _kv_buf_len()
    new_pool = []
    kk = vv = new_k = new_v = None
    for i in range(DEPTH):
        h = _rmsnorm(x, g_mix[i])
        if i < N_A:
            x = x + _pool_mixer(h, pool_prev[i], pool_mask, w_pool[i], pool_scale[i])
            new_pool.append(jnp.concatenate([pool_prev[i].astype(h.dtype), h], axis=1)[:, -POOL_BUF:])
        else:
            j = i - N_A
            q = (h @ w_q[j] + b_q[j]).reshape(b, t, N_KV_HEADS, GROUP, HEAD_DIM)
            if prompt:
                o = _swa_prompt(q, kk, vv, sinks[j])
            else:
                o = _swa_sample(q, kk, vv, sinks[j])
            x = x + o @ w_o[j]
        h = _rmsnorm(x, g_ffn[i])
        if i % 2 == 0:
            x = x + _swiglu(h, w1_dense[i // 2], w3_dense[i // 2], w2_dense[i // 2])
        else:
            x = x + _moe(h, w_router[i // 2], b_router[i // 2],
                         w1_moe[i // 2], w3_moe[i // 2], w2_moe[i // 2])
        e_ple = p[i].astype(x.dtype) @ w_ple_proj[i]
        gate = jax.nn.sigmoid(_rmsnorm(x, g_ple[i]) @ w_ple_gate[i])
        x = x + gate * e_ple
        if i == N_A - 1:
            kv_in = _rmsnorm(x, g_kv)
            k = (kv_in @ w_k + b_k).reshape(b, t, N_KV_HEADS, HEAD_DIM)
            v = (kv_in @ w_v + b_v).reshape(b, t, N_KV_HEADS, HEAD_DIM)
            if prompt:
                kk, vv = k, v
            else:
                kk = jnp.concatenate([cache_k.astype(k.dtype), k], axis=1)
                vv = jnp.concatenate([cache_v.astype(v.dtype), v], axis=1)
            new_k = kk[:, -kv_buf:]
            new_v = vv[:, -kv_buf:]
    return _rmsnorm(x, g_final), jnp.stack(new_pool, axis=0), new_k, new_v


def setup_inputs(seed: int = 0) -> dict:
    key = jax.random.key(seed)
    ks = iter(jax.random.split(key, 40))
    f32 = jnp.float32

    def nrm(shape, scale=1.0):
        return jax.random.normal(next(ks), shape, f32) * scale

    kv_buf = _kv_buf_len()
    kvw = N_KV_HEADS * HEAD_DIM
    qw = N_HEADS * HEAD_DIM
    return {
        'x_prompt': nrm((BATCH, SEQ, D_MODEL)),
        'x_sample': nrm((DEC_BATCH, DEC_SEQ, D_MODEL)),
        'state_pool': nrm((N_A, DEC_BATCH, POOL_BUF, D_MODEL)),
        'cache_k': nrm((DEC_BATCH, kv_buf, N_KV_HEADS, HEAD_DIM)),
        'cache_v': nrm((DEC_BATCH, kv_buf, N_KV_HEADS, HEAD_DIM)),
        'p_prompt': nrm((DEPTH, BATCH, SEQ, PLE_DIM)),
        'p_sample': nrm((DEPTH, DEC_BATCH, DEC_SEQ, PLE_DIM)),
        'g_mix': 1.0 + nrm((DEPTH, D_MODEL), 0.05),
        'g_ffn': 1.0 + nrm((DEPTH, D_MODEL), 0.05),
        'g_ple': 1.0 + nrm((DEPTH, D_MODEL), 0.05),
        'g_kv': 1.0 + nrm((D_MODEL,), 0.05),
        'g_final': 1.0 + nrm((D_MODEL,), 0.05),
        'w_pool': nrm((N_A, N_POOL_GROUPS, POOL_GW, POOL_GW), POOL_GW ** -0.5),
        'pool_scale': 1.0 + nrm((N_A, D_MODEL), 0.1),
        'w_q': nrm((N_B, D_MODEL, qw), D_MODEL ** -0.5),
        'b_q': nrm((N_B, qw), 0.02),
        'w_k': nrm((D_MODEL, kvw), D_MODEL ** -0.5),
        'b_k': nrm((kvw,), 0.02),
        'w_v': nrm((D_MODEL, kvw), D_MODEL ** -0.5),
        'b_v': nrm((kvw,), 0.02),
        'w_o': nrm((N_B, qw, D_MODEL), qw ** -0.5),
        'sinks': nrm((N_B, N_HEADS), 0.5),
        'w1_dense': nrm((N_DENSE, D_MODEL, D_FF), D_MODEL ** -0.5),
        'w3_dense': nrm((N_DENSE, D_MODEL, D_FF), D_MODEL ** -0.5),
        'w2_dense': nrm((N_DENSE, D_FF, D_MODEL), D_FF ** -0.5),
        'w_router': nrm((N_MOE, D_MODEL, N_EXPERTS), D_MODEL ** -0.5),
        'b_router': nrm((N_MOE, N_EXPERTS), 0.01),
        'w1_moe': nrm((N_MOE, N_EXPERTS, D_MODEL, D_FF), D_MODEL ** -0.5),
        'w3_moe': nrm((N_MOE, N_EXPERTS, D_MODEL, D_FF), D_MODEL ** -0.5),
        'w2_moe': nrm((N_MOE, N_EXPERTS, D_FF, D_MODEL), D_FF ** -0.5),
        'w_ple_proj': nrm((DEPTH, PLE_DIM, D_MODEL), PLE_DIM ** -0.5),
        'w_ple_gate': nrm((DEPTH, D_MODEL, D_MODEL), D_MODEL ** -0.5),
    }


def reference(x_prompt, x_sample, state_pool, cache_k, cache_v, p_prompt, p_sample,
              g_mix, g_ffn, g_ple, g_kv, g_final, w_pool, pool_scale,
              w_q, b_q, w_k, b_k, w_v, b_v, w_o, sinks,
              w1_dense, w3_dense, w2_dense, w_router, b_router, w1_moe, w3_moe, w2_moe,
              w_ple_proj, w_ple_gate):
    weights = (g_mix, g_ffn, g_ple, g_kv, g_final, w_pool, pool_scale,
               w_q, b_q, w_k, b_k, w_v, b_v, w_o, sinks,
               w1_dense, w3_dense, w2_dense, w_router, b_router, w1_moe, w3_moe, w2_moe,
               w_ple_proj, w_ple_gate)
    bp, tp = x_prompt.shape[0], x_prompt.shape[1]
    ts = x_sample.shape[1]
    prev_p = jnp.zeros((N_A, bp, POOL_BUF, D_MODEL), x_prompt.dtype)
    mask_p = jnp.concatenate([jnp.zeros((POOL_BUF,), jnp.float32), jnp.ones((tp,), jnp.float32)])
    y_prompt, pool_p, k_p, v_p = _trunk(x_prompt, p_prompt, prev_p, mask_p, None, None, *weights)
    mask_s = jnp.ones((POOL_BUF + ts,), jnp.float32)
    y_sample, pool_s, k_s, v_s = _trunk(x_sample, p_sample, state_pool, mask_s, cache_k, cache_v, *weights)
    return (y_prompt, y_sample, pool_p, pool_s, k_p, v_p, k_s, v_s)
```

```python
import functools

import jax
import jax.numpy as jnp
from jax import lax
from jax.experimental import pallas as pl
from jax.experimental.pallas import tpu as pltpu

EPS = 1e-6
NEG = -1e30
CHUNK = 64
N_BACK = 2
POOL_WINDOWS = (2, 4, 8, 16)
POOL_BUF = 15
HIST = 16
TOP_K = 2
ROUTE_LANES = 128
VMEM_LIMIT = 56 * 1024 * 1024

F32 = jnp.float32
BF16 = jnp.bfloat16


def _rms(x, g):
    return (x * lax.rsqrt(jnp.mean(x * x, axis=-1, keepdims=True) + EPS)) * g


def _dot(a, b):
    return jnp.dot(a, b, preferred_element_type=F32)


def _const_spec(shape):
    nd = len(shape)
    return pl.BlockSpec(shape, lambda *_: (0,) * nd, pipeline_mode=pl.Buffered(1))


def _params(n_axes):
    return pltpu.CompilerParams(dimension_semantics=("arbitrary",) * n_axes, vmem_limit_bytes=VMEM_LIMIT)


def _pool_front_kernel(*refs, prompt, nb, ta, d):
    (x_ref, hist_ref, gmix_ref, gffn_ref, wpool_ref, scale_ref,
     x1_ref, hf_ref, state_ref, e_ref, f2_ref, f4_ref, f8_ref) = refs
    gw = d // len(POOL_WINDOWS)
    j = pl.program_id(1)
    n = ta + HIST
    x = x_ref[...]
    g = gmix_ref[...]
    h = _rms(x, g)
    if prompt:
        hp = _rms(hist_ref[...], g)
        hp = jnp.where(j == 0, 0.0, hp)
        e_ref[:, 0:HIST, :] = hp
    else:
        e_ref[:, 0:1, :] = jnp.zeros((nb, 1, d), F32)
        e_ref[:, 1:HIST, :] = hist_ref[...]
    e_ref[:, HIST:n, :] = h
    zero_tail = jnp.zeros((nb, 8, d), F32)
    e_ref[:, n:n + 8, :] = zero_tail
    f2_ref[:, 0:n, :] = e_ref[:, 0:n, :] + e_ref[:, 1:n + 1, :]
    f2_ref[:, n:n + 8, :] = zero_tail
    f4_ref[:, 0:n, :] = f2_ref[:, 0:n, gw:] + f2_ref[:, 2:n + 2, gw:]
    f4_ref[:, n:n + 8, :] = zero_tail[:, :, gw:]
    f8_ref[:, 0:n, :] = f4_ref[:, 0:n, gw:] + f4_ref[:, 4:n + 4, gw:]
    f8_ref[:, n:n + 8, :] = zero_tail[:, :, 2 * gw:]
    sums = (
        f2_ref[:, HIST - 1:HIST - 1 + ta, 0:gw],
        f4_ref[:, HIST - 3:HIST - 3 + ta, 0:gw],
        f8_ref[:, HIST - 7:HIST - 7 + ta, 0:gw],
        f8_ref[:, HIST - 15:HIST - 15 + ta, gw:] + f8_ref[:, HIST - 7:HIST - 7 + ta, gw:],
    )
    t_idx = lax.broadcasted_iota(jnp.int32, (1, ta, 1), 1) + j * ta
    ys = []
    for gi, w in enumerate(POOL_WINDOWS):
        if prompt:
            cnt = jnp.minimum(t_idx + 1, w).astype(F32)
            mean = sums[gi] / cnt
        else:
            mean = sums[gi] * (1.0 / w)
        diff = mean - h[:, :, gi * gw:(gi + 1) * gw]
        ys.append(_dot(diff.reshape(nb * ta, gw).astype(BF16), wpool_ref[gi]))
    y = jnp.concatenate(ys, axis=-1) * scale_ref[...]
    x1 = x.reshape(nb * ta, d) + y
    x1_ref[...] = x1
    hf_ref[...] = _rms(x1, gffn_ref[...]).astype(BF16)
    if prompt:
        @pl.when(j == pl.num_programs(1) - 1)
        def _():
            state_ref[0] = e_ref[:, n - POOL_BUF:n, :]
    else:
        state_ref[0] = e_ref[:, n - POOL_BUF:n, :]


def _pool_front(x_p, x_s, state, gmix, gffn, wpool, scale, *, ta, nbs):
    bp, sp, d = x_p.shape
    bs, ss, _ = x_s.shape
    tp, ts = bp * sp, bs * ss
    gw = d // len(POOL_WINDOWS)
    assert sp % ta == 0 and ta % HIST == 0 and bs % nbs == 0
    vec = pl.BlockSpec((1, d), lambda *_: (0, 0))
    wspec = pl.BlockSpec((len(POOL_WINDOWS), gw, gw), lambda *_: (0, 0, 0))

    def scratch(nb, rows):
        return [pltpu.VMEM((nb, rows + HIST + 8, d), F32), pltpu.VMEM((nb, rows + HIST + 8, d), F32),
                pltpu.VMEM((nb, rows + HIST + 8, d - gw), F32), pltpu.VMEM((nb, rows + HIST + 8, d - 2 * gw), F32)]

    njp = sp // ta
    x1_p, hf_p, state_p = pl.pallas_call(
        functools.partial(_pool_front_kernel, prompt=True, nb=1, ta=ta, d=d),
        grid=(bp, njp),
        in_specs=[
            pl.BlockSpec((1, ta, d), lambda b, j: (b, j, 0)),
            pl.BlockSpec((1, HIST, d), lambda b, j: (b, jnp.maximum(j * (ta // HIST) - 1, 0), 0)),
            vec, vec, wspec, vec,
        ],
        out_specs=[
            pl.BlockSpec((ta, d), lambda b, j: (b * njp + j, 0)),
            pl.BlockSpec((ta, d), lambda b, j: (b * njp + j, 0)),
            pl.BlockSpec((1, 1, POOL_BUF, d), lambda b, j: (0, b, 0, 0)),
        ],
        out_shape=[jax.ShapeDtypeStruct((tp, d), F32), jax.ShapeDtypeStruct((tp, d), BF16),
                   jax.ShapeDtypeStruct((1, bp, POOL_BUF, d), F32)],
        scratch_shapes=scratch(1, ta),
        compiler_params=_params(2),
        name="pool_front_prompt",
    )(x_p, x_p, gmix, gffn, wpool, scale)

    rows = nbs * ss
    x1_s, hf_s, state_s = pl.pallas_call(
        functools.partial(_pool_front_kernel, prompt=False, nb=nbs, ta=ss, d=d),
        grid=(bs // nbs, 1),
        in_specs=[
            pl.BlockSpec((nbs, ss, d), lambda b, j: (b, 0, 0)),
            pl.BlockSpec((nbs, POOL_BUF, d), lambda b, j: (b, 0, 0)),
            vec, vec, wspec, vec,
        ],
        out_specs=[
            pl.BlockSpec((rows, d), lambda b, j: (b, 0)),
            pl.BlockSpec((rows, d), lambda b, j: (b, 0)),
            pl.BlockSpec((1, nbs, POOL_BUF, d), lambda b, j: (0, b, 0, 0)),
        ],
        out_shape=[jax.ShapeDtypeStruct((ts, d), F32), jax.ShapeDtypeStruct((ts, d), BF16),
                   jax.ShapeDtypeStruct((1, bs, POOL_BUF, d), F32)],
        scratch_shapes=scratch(nbs, ss),
        compiler_params=_params(2),
        name="pool_front_sample",
    )(x_s, state, gmix, gffn, wpool, scale)
    return x1_p, x1_s, hf_p, hf_s, state_p, state_s


def _swiglu_acc(x_bf, w1_ref, w3_ref, w2_ref, acc_ref, fc):
    d_ff = w1_ref.shape[-1]
    for c in range(d_ff // fc):
        sl = slice(c * fc, (c + 1) * fc)
        h1 = _dot(x_bf, w1_ref[:, sl])
        h3 = _dot(x_bf, w3_ref[:, sl])
        a = (h1 * jax.nn.sigmoid(h1)) * h3
        acc_ref[...] += _dot(a.astype(BF16), w2_ref[sl, :])


def _ple_gate(x, p_bf, gple, wproj_ref, wgate_ref):
    e = _dot(p_bf, wproj_ref[...])
    gate = jax.nn.sigmoid(_dot(_rms(x, gple).astype(BF16), wgate_ref[...]))
    return x + gate * e


def _ffn0_post_kernel(hfp_ref, hfs_ref, x1p_ref, x1s_ref, pp_ref, ps_ref, w1_ref, w3_ref, w2_ref,
                      gple_ref, wproj_ref, wgate_ref, gkv_ref, wk_ref, bk_ref, wv_ref, bv_ref,
                      gmix_ref, wq_ref, bq_ref,
                      x3_ref, k_ref, v_ref, q_ref, acc_ref, *, fc, n_prompt_tiles):
    is_prompt = pl.program_id(0) < n_prompt_tiles
    acc_ref[...] = jnp.where(is_prompt, x1p_ref[...], x1s_ref[...])
    _swiglu_acc(jnp.where(is_prompt, hfp_ref[...], hfs_ref[...]), w1_ref, w3_ref, w2_ref, acc_ref, fc)
    p = jnp.where(is_prompt, pp_ref[...], ps_ref[...]).astype(BF16)
    x3 = _ple_gate(acc_ref[...], p, gple_ref[...], wproj_ref, wgate_ref)
    x3_ref[...] = x3
    kv_in = _rms(x3, gkv_ref[...]).astype(BF16)
    k_ref[...] = _dot(kv_in, wk_ref[...]) + bk_ref[...]
    v_ref[...] = _dot(kv_in, wv_ref[...]) + bv_ref[...]
    hq = _rms(x3, gmix_ref[...]).astype(BF16)
    q_ref[...] = (_dot(hq, wq_ref[...]) + bq_ref[...]).astype(BF16)


def _split_specs(tm, width, n_prompt_tiles):
    return [pl.BlockSpec((tm, width), lambda i, *_: (jnp.minimum(i, n_prompt_tiles - 1), 0)),
            pl.BlockSpec((tm, width), lambda i, *_: (jnp.maximum(i - n_prompt_tiles, 0), 0))]


def _ffn0_post(hf_p, hf_s, x1_p, x1_s, pp, ps, w1, w3, w2, gple, wproj, wgate, gkv, wk, bk, wv, bv, gmix, wq, bq,
               *, tm, fc):
    tp, d = x1_p.shape
    ts = x1_s.shape[0]
    t = tp + ts
    kvw = wk.shape[1]
    assert ts % tm == 0 and tp % tm == 0 and w1.shape[1] % fc == 0
    npt = tp // tm
    row = lambda w: pl.BlockSpec((tm, w), lambda i: (i, 0))
    vec = lambda w: pl.BlockSpec((1, w), lambda i: (0, 0))
    return pl.pallas_call(
        functools.partial(_ffn0_post_kernel, fc=fc, n_prompt_tiles=npt),
        grid=(t // tm,),
        in_specs=_split_specs(tm, d, npt) + _split_specs(tm, d, npt) + _split_specs(tm, pp.shape[1], npt) + [
            _const_spec(w1.shape), _const_spec(w3.shape), _const_spec(w2.shape),
            vec(d), _const_spec(wproj.shape), _const_spec(wgate.shape),
            vec(d), _const_spec(wk.shape), vec(kvw), _const_spec(wv.shape), vec(kvw),
            vec(d), _const_spec(wq.shape), vec(d),
        ],
        out_specs=[row(d), row(kvw), row(kvw), row(d)],
        out_shape=[jax.ShapeDtypeStruct((t, d), F32), jax.ShapeDtypeStruct((t, kvw), F32),
                   jax.ShapeDtypeStruct((t, kvw), F32), jax.ShapeDtypeStruct((t, d), BF16)],
        scratch_shapes=[pltpu.VMEM((tm, d), F32)],
        compiler_params=_params(1),
        name="ffn0_post",
    )(hf_p, hf_s, x1_p, x1_s, pp, ps, w1, w3, w2, gple, wproj, wgate, gkv, wk, bk, wv, bv, gmix, wq, bq)


def _swa_kernel(*refs, prompt, tq, n_kv, group, hd):
    if prompt:
        sink_ref, q_ref, ka_ref, kb_ref, va_ref, vb_ref, o_ref = refs
        kk = jnp.concatenate([ka_ref[0], kb_ref[0]], axis=0).astype(BF16)
        vv = jnp.concatenate([va_ref[0], vb_ref[0]], axis=0).astype(BF16)
    else:
        sink_ref, q_ref, k_ref, v_ref, o_ref = refs
        kk = k_ref[0].astype(BF16)
        vv = v_ref[0].astype(BF16)
    nk = kk.shape[0]
    q = q_ref[...]
    if prompt:
        j = pl.program_id(1)
        qc = (lax.broadcasted_iota(jnp.int32, (group * tq, nk), 0) % tq) // CHUNK
        kc = lax.broadcasted_iota(jnp.int32, (group * tq, nk), 1) // CHUNK
        lo = jnp.maximum(qc, N_BACK - j * (tq // CHUNK))
        bias = jnp.where(kc >= lo, jnp.where(kc <= qc + N_BACK, 0.0, NEG), NEG)
    outs = []
    for kvh in range(n_kv):
        kh = kk[:, kvh * hd:(kvh + 1) * hd]
        vh = vv[:, kvh * hd:(kvh + 1) * hd]
        heads = [kvh * group + gi for gi in range(group)]
        qs = jnp.concatenate([q[:, hh * hd:(hh + 1) * hd] for hh in heads], axis=0)
        s = lax.dot_general(qs, kh, (((1,), (1,)), ((), ())), preferred_element_type=F32) * (hd ** -0.5)
        if prompt:
            s = jnp.where(bias < 0.0, NEG, s)
        sink = jnp.concatenate([jnp.full((tq, 1), sink_ref[hh], F32) for hh in heads], axis=0)
        m = jnp.maximum(jnp.max(s, axis=-1, keepdims=True), sink)
        p = jnp.exp(s - m)
        den = jnp.sum(p, axis=-1, keepdims=True) + jnp.exp(sink - m)
        o = _dot(p.astype(BF16), vh) / den
        outs += [o[gi * tq:(gi + 1) * tq] for gi in range(group)]
    lanes = 128
    per = lanes // hd
    for c in range(len(outs) // per):
        o_ref[:, c * lanes:(c + 1) * lanes] = jnp.concatenate(outs[c * per:(c + 1) * per], axis=-1).astype(BF16)


def _swa(q, k, v, cache_k, cache_v, sinks, *, bp, sp, bs, ss, tq):
    t, d = q.shape
    kvw = k.shape[1]
    tp = bp * sp
    n_heads = sinks.shape[0]
    hd = d // n_heads
    n_kv = kvw // hd
    back = N_BACK * CHUNK
    assert tq == back and sp % tq == 0 and ss == CHUNK and cache_k.shape[1] == back
    kp = jnp.pad(k[:tp].reshape(bp, sp, kvw), ((0, 0), (back, 0), (0, 0)))
    vp = jnp.pad(v[:tp].reshape(bp, sp, kvw), ((0, 0), (back, 0), (0, 0)))
    ks = jnp.concatenate([cache_k.reshape(bs, back, kvw), k[tp:].reshape(bs, ss, kvw)], axis=1)
    vs = jnp.concatenate([cache_v.reshape(bs, back, kvw), v[tp:].reshape(bs, ss, kvw)], axis=1)
    smem = pl.BlockSpec(memory_space=pltpu.SMEM)
    nj = sp // tq
    kw = dict(n_kv=n_kv, group=n_heads // n_kv, hd=hd)
    o_p = pl.pallas_call(
        functools.partial(_swa_kernel, prompt=True, tq=tq, **kw),
        grid=(bp, nj),
        in_specs=[smem, pl.BlockSpec((tq, d), lambda b, j: (b * nj + j, 0)),
                  pl.BlockSpec((1, tq, kvw), lambda b, j: (b, j, 0)),
                  pl.BlockSpec((1, tq, kvw), lambda b, j: (b, j + 1, 0)),
                  pl.BlockSpec((1, tq, kvw), lambda b, j: (b, j, 0)),
                  pl.BlockSpec((1, tq, kvw), lambda b, j: (b, j + 1, 0))],
        out_specs=pl.BlockSpec((tq, d), lambda b, j: (b * nj + j, 0)),
        out_shape=jax.ShapeDtypeStruct((tp, d), BF16),
        compiler_params=_params(2),
        name="swa_prompt",
    )(sinks, q, kp, kp, vp, vp)
    off = tp // ss
    o_s = pl.pallas_call(
        functools.partial(_swa_kernel, prompt=False, tq=ss, **kw),
        grid=(bs,),
        in_specs=[smem, pl.BlockSpec((ss, d), lambda b: (off + b, 0)),
                  pl.BlockSpec((1, back + ss, kvw), lambda b: (b, 0, 0)),
                  pl.BlockSpec((1, back + ss, kvw), lambda b: (b, 0, 0))],
        out_specs=pl.BlockSpec((ss, d), lambda b: (b, 0)),
        out_shape=jax.ShapeDtypeStruct((bs * ss, d), BF16),
        compiler_params=_params(1),
        name="swa_sample",
    )(sinks, q, ks, vs)
    return o_p, o_s, kp, vp, ks, vs


def _oproj_route_kernel(op_ref, os_ref, x3_ref, wo_ref, gffn_ref, wr_ref, br_ref, x4_ref, hm_ref, route_ref,
                        *, n_prompt_tiles):
    o = jnp.where(pl.program_id(0) < n_prompt_tiles, op_ref[...], os_ref[...])
    x4 = x3_ref[...] + _dot(o, wo_ref[...])
    x4_ref[...] = x4
    hm = _rms(x4, gffn_ref[...])
    hm_ref[...] = hm
    logits = jnp.dot(hm, wr_ref[...], preferred_element_type=F32, precision=lax.Precision.HIGHEST) + br_ref[...]
    lane = lax.broadcasted_iota(jnp.int32, logits.shape, 1).astype(F32)
    m1 = jnp.max(logits, axis=-1, keepdims=True)
    i1 = jnp.min(jnp.where(logits == m1, lane, float(ROUTE_LANES)), axis=-1, keepdims=True)
    rest = jnp.where(lane == i1, -jnp.inf, logits)
    m2 = jnp.max(rest, axis=-1, keepdims=True)
    i2 = jnp.min(jnp.where(rest == m2, lane, float(ROUTE_LANES)), axis=-1, keepdims=True)
    e2 = jnp.exp(m2 - m1)
    g1 = 1.0 / (1.0 + e2)
    g2 = e2 / (1.0 + e2)
    route = jnp.where(lane == 0.0, i1, jnp.where(lane == 1.0, i2,
                      jnp.where(lane == 2.0, g1, jnp.where(lane == 3.0, g2, 0.0))))
    route_ref[...] = route


def _oproj_route(o_p, o_s, x3, wo, gffn, wr, br, *, tm):
    t, d = x3.shape
    tp = o_p.shape[0]
    assert tp % tm == 0 and o_s.shape[0] % tm == 0
    row = lambda w: pl.BlockSpec((tm, w), lambda i: (i, 0))
    return pl.pallas_call(
        functools.partial(_oproj_route_kernel, n_prompt_tiles=tp // tm),
        grid=(t // tm,),
        in_specs=_split_specs(tm, d, tp // tm) + [row(d), _const_spec(wo.shape), pl.BlockSpec((1, d), lambda i: (0, 0)),
                  _const_spec(wr.shape), pl.BlockSpec((1, ROUTE_LANES), lambda i: (0, 0))],
        out_specs=[row(d), row(d), row(ROUTE_LANES)],
        out_shape=[jax.ShapeDtypeStruct((t, d), F32), jax.ShapeDtypeStruct((t, d), F32),
                   jax.ShapeDtypeStruct((t, ROUTE_LANES), F32)],
        compiler_params=_params(1),
        name="oproj_route",
    )(o_p, o_s, x3, wo, gffn, wr, br)


def _moe_kernel(texp_ref, nval_ref, rtok_ref, rdst_ref, hm_ref, w1_ref, w3_ref, w2_ref, y_ref,
                xbuf, obuf, acc_ref, gsem, ssem, *, tm, fc):
    i = pl.program_id(0)
    nt = pl.num_programs(0)
    slot = i % 2

    def gather_copy(tile, r, sl):
        tok = rtok_ref[tile * tm + r]
        return pltpu.make_async_copy(hm_ref.at[pl.ds(tok, 1), :], xbuf.at[sl, pl.ds(r, 1), :], gsem.at[sl])

    def scatter_copy(tile, r, sl):
        dst = rdst_ref[tile * tm + r]
        return pltpu.make_async_copy(obuf.at[sl, pl.ds(r, 1), :], y_ref.at[pl.ds(dst, 1), :], ssem.at[sl])

    def start_gather(tile, sl):
        lax.fori_loop(0, nval_ref[tile], lambda r, c: (gather_copy(tile, r, sl).start(), c)[1], 0)

    def wait_scatter(tile, sl):
        lax.fori_loop(0, nval_ref[tile], lambda r, c: (scatter_copy(tile, r, sl).wait(), c)[1], 0)

    @pl.when(i == 0)
    def _():
        xbuf[...] = jnp.zeros(xbuf.shape, F32)
        start_gather(0, 0)

    @pl.when(i + 1 < nt)
    def _():
        start_gather(i + 1, 1 - slot)

    nv = nval_ref[i]
    lax.fori_loop(0, nv, lambda r, c: (gather_copy(i, r, slot).wait(), c)[1], 0)

    @pl.when(i >= 2)
    def _():
        wait_scatter(i - 2, slot)

    @pl.when(nv > 0)
    def _():
        acc_ref[...] = jnp.zeros(acc_ref.shape, F32)
        _swiglu_acc(xbuf[slot].astype(BF16), w1_ref, w3_ref, w2_ref, acc_ref, fc)
        obuf[slot] = acc_ref[...]
        lax.fori_loop(0, nv, lambda r, c: (scatter_copy(i, r, slot).start(), c)[1], 0)

    @pl.when(i == nt - 1)
    def _():
        @pl.when(nt >= 2)
        def _():
            wait_scatter(i - 1, 1 - slot)
        wait_scatter(i, slot)


def _moe(hm, route, w1, w3, w2, *, tm, fc):
    t, d = hm.shape
    n_exp = w1.shape[0]
    n_assign = t * TOP_K
    nt = n_assign // tm + n_exp
    assert n_assign % tm == 0
    e_flat = route[:, :TOP_K].astype(jnp.int32).reshape(-1)
    order = jnp.argsort(e_flat, stable=True).astype(jnp.int32)
    counts = jnp.sum((e_flat[:, None] == jnp.arange(n_exp, dtype=jnp.int32)[None, :]).astype(jnp.int32), axis=0)
    tiles_e = (counts + tm - 1) // tm
    tile_end = jnp.cumsum(tiles_e)
    tile_start = tile_end - tiles_e
    cnt_start = jnp.cumsum(counts) - counts
    tile_ids = jnp.arange(nt, dtype=jnp.int32)
    texp = jnp.minimum(jnp.searchsorted(tile_end, tile_ids, side="right"), n_exp - 1).astype(jnp.int32)
    within = (tile_ids - tile_start[texp]) * tm
    nval = jnp.where(tile_ids < tile_end[-1], jnp.clip(counts[texp] - within, 0, tm), 0).astype(jnp.int32)
    r_in = jnp.arange(tm, dtype=jnp.int32)[None, :]
    src = jnp.clip(cnt_start[texp][:, None] + within[:, None] + r_in, 0, n_assign - 1)
    assign = jnp.where(r_in < nval[:, None], order[src], 0)
    rtok = (assign // TOP_K).reshape(-1)
    rdst = ((assign % TOP_K) * t + assign // TOP_K).reshape(-1)

    wspec = lambda shape: pl.BlockSpec((None,) + shape[1:], lambda i, te, *_: (te[i], 0, 0),
                                       pipeline_mode=pl.Buffered(1))
    y2 = pl.pallas_call(
        functools.partial(_moe_kernel, tm=tm, fc=fc),
        grid_spec=pltpu.PrefetchScalarGridSpec(
            num_scalar_prefetch=4,
            grid=(nt,),
            in_specs=[pl.BlockSpec(memory_space=pl.ANY), wspec(w1.shape), wspec(w3.shape), wspec(w2.shape)],
            out_specs=pl.BlockSpec(memory_space=pl.ANY),
            scratch_shapes=[pltpu.VMEM((2, tm, d), F32), pltpu.VMEM((2, tm, d), F32), pltpu.VMEM((tm, d), F32),
                            pltpu.SemaphoreType.DMA((2,)), pltpu.SemaphoreType.DMA((2,))],
        ),
        out_shape=jax.ShapeDtypeStruct((TOP_K * t, d), F32),
        compiler_params=_params(1),
        name="moe",
    )(texp, nval, rtok, rdst, hm, w1, w3, w2)
    return y2


def _final_kernel(x4_ref, ya_ref, yb_ref, route_ref, pp_ref, ps_ref, gple_ref, wproj_ref, wgate_ref, gfin_ref,
                  yp_ref, ys_ref, *, n_prompt_tiles):
    i = pl.program_id(0)
    route = route_ref[...]
    g1 = route[:, 2:3]
    g2 = route[:, 3:4]
    x5 = x4_ref[...] + (g1 * ya_ref[...] + g2 * yb_ref[...])
    p = jnp.where(i < n_prompt_tiles, pp_ref[...], ps_ref[...]).astype(BF16)
    x6 = _ple_gate(x5, p, gple_ref[...], wproj_ref, wgate_ref)
    y = _rms(x6, gfin_ref[...])

    @pl.when(i < n_prompt_tiles)
    def _():
        yp_ref[...] = y

    @pl.when(i >= n_prompt_tiles)
    def _():
        ys_ref[...] = y


def _final(x4, y2, route, pp, ps, gple, wproj, wgate, gfin, *, tm):
    t, d = x4.shape
    tp, ts = pp.shape[0], ps.shape[0]
    npt = tp // tm
    assert tp % tm == 0 and ts % tm == 0
    row = lambda w: pl.BlockSpec((tm, w), lambda i: (i, 0))
    vec = pl.BlockSpec((1, d), lambda i: (0, 0))
    return pl.pallas_call(
        functools.partial(_final_kernel, n_prompt_tiles=npt),
        grid=(t // tm,),
        in_specs=[row(d), row(d), pl.BlockSpec((tm, d), lambda i: (t // tm + i, 0)), row(ROUTE_LANES)]
        + _split_specs(tm, pp.shape[1], npt) + [vec, _const_spec(wproj.shape), _const_spec(wgate.shape), vec],
        out_specs=_split_specs(tm, d, npt),
        out_shape=[jax.ShapeDtypeStruct((tp, d), F32), jax.ShapeDtypeStruct((ts, d), F32)],
        compiler_params=_params(1),
        name="final",
    )(x4, y2, y2, route, pp, ps, gple, wproj, wgate, gfin)


def kernel(x_prompt, x_sample, state_pool, cache_k, cache_v, p_prompt, p_sample, g_mix, g_ffn, g_ple, g_kv, g_final,
           w_pool, pool_scale, w_q, b_q, w_k, b_k, w_v, b_v, w_o, sinks, w1_dense, w3_dense, w2_dense, w_router,
           b_router, w1_moe, w3_moe, w2_moe, w_ple_proj, w_ple_gate):
    bp, sp, d = x_prompt.shape
    bs, ss, _ = x_sample.shape
    depth = g_mix.shape[0]
    assert depth == 2 and w_pool.shape[0] == 1 and w_q.shape[0] == 1 and w1_dense.shape[0] == 1 and w1_moe.shape[0] == 1
    tp, ts = bp * sp, bs * ss
    n_kv, hd = cache_k.shape[2], cache_k.shape[3]
    n_exp = w_router.shape[2]
    ple = p_prompt.shape[-1]
    bf = lambda w: w.astype(BF16)
    vec = lambda g: g.reshape(1, -1)

    x1_p, x1_s, hf_p, hf_s, pool_p, pool_s = _pool_front(
        x_prompt, x_sample, state_pool[0], vec(g_mix[0]), vec(g_ffn[0]), bf(w_pool[0]), vec(pool_scale[0]),
        ta=512, nbs=8)

    x3, k, v, q = _ffn0_post(
        hf_p, hf_s, x1_p, x1_s, p_prompt[0].reshape(tp, ple), p_sample[0].reshape(ts, ple),
        bf(w1_dense[0]), bf(w3_dense[0]), bf(w2_dense[0]),
        vec(g_ple[0]), bf(w_ple_proj[0]), bf(w_ple_gate[0]),
        vec(g_kv), bf(w_k), vec(b_k), bf(w_v), vec(b_v),
        vec(g_mix[1]), bf(w_q[0]), vec(b_q[0]), tm=512, fc=512)

    o_p, o_s, kp, vp, ks, vs = _swa(q, k, v, cache_k, cache_v, sinks[0], bp=bp, sp=sp, bs=bs, ss=ss,
                                    tq=N_BACK * CHUNK)

    wr = jnp.pad(w_router[0], ((0, 0), (0, ROUTE_LANES - n_exp)))
    br = jnp.pad(b_router[0], (0, ROUTE_LANES - n_exp), constant_values=NEG).reshape(1, ROUTE_LANES)
    x4, hm, route = _oproj_route(o_p, o_s, x3, bf(w_o[0]), vec(g_ffn[1]), wr, br, tm=512)

    y2 = _moe(hm, route, bf(w1_moe[0]), bf(w3_moe[0]), bf(w2_moe[0]), tm=256, fc=512)

    y_p, y_s = _final(x4, y2, route, p_prompt[1].reshape(tp, ple), p_sample[1].reshape(ts, ple),
                      vec(g_ple[1]), bf(w_ple_proj[1]), bf(w_ple_gate[1]), vec(g_final), tm=512)

    back = N_BACK * CHUNK
    return (y_p.reshape(bp, sp, d), y_s.reshape(bs, ss, d), pool_p, pool_s,
            kp[:, -back:].reshape(bp, back, n_kv, hd), vp[:, -back:].reshape(bp, back, n_kv, hd),
            ks[:, -back:].reshape(bs, back, n_kv, hd), vs[:, -back:].reshape(bs, back, n_kv, hd))
```

```python
import functools

import jax
import jax.numpy as jnp
from jax import lax
from jax.experimental import pallas as pl
from jax.experimental.pallas import tpu as pltpu

EPS = 1e-6
NEG = -1e30
CHUNK = 64
N_BACK = 2
POOL_WINDOWS = (2, 4, 8, 16)
POOL_BUF = 15
HIST = 16
TOP_K = 2
ROUTE_LANES = 128
VMEM_LIMIT = 56 * 1024 * 1024

F32 = jnp.float32
BF16 = jnp.bfloat16


def _rms(x, g):
    return (x * lax.rsqrt(jnp.mean(x * x, axis=-1, keepdims=True) + EPS)) * g


def _dot(a, b):
    return jnp.dot(a, b, preferred_element_type=F32)


def _const_spec(shape):
    nd = len(shape)
    return pl.BlockSpec(shape, lambda *_: (0,) * nd, pipeline_mode=pl.Buffered(1))


def _params(n_axes):
    return pltpu.CompilerParams(dimension_semantics=("arbitrary",) * n_axes, vmem_limit_bytes=VMEM_LIMIT)


def _pool_front_kernel(*refs, prompt, nb, ta, d):
    (x_ref, hist_ref, gmix_ref, gffn_ref, wpool_ref, scale_ref,
     x1_ref, hf_ref, state_ref, e_ref, f2_ref, f4_ref, f8_ref) = refs
    gw = d // len(POOL_WINDOWS)
    j = pl.program_id(1)
    n = ta + HIST
    x = x_ref[...]
    g = gmix_ref[...]
    h = _rms(x, g)
    if prompt:
        hp = _rms(hist_ref[...], g)
        hp = jnp.where(j == 0, 0.0, hp)
        e_ref[:, 0:HIST, :] = hp
    else:
        e_ref[:, 0:1, :] = jnp.zeros((nb, 1, d), F32)
        e_ref[:, 1:HIST, :] = hist_ref[...]
    e_ref[:, HIST:n, :] = h
    zero_tail = jnp.zeros((nb, 8, d), F32)
    e_ref[:, n:n + 8, :] = zero_tail
    f2_ref[:, 0:n, :] = e_ref[:, 0:n, :] + e_ref[:, 1:n + 1, :]
    f2_ref[:, n:n + 8, :] = zero_tail
    f4_ref[:, 0:n, :] = f2_ref[:, 0:n, gw:] + f2_ref[:, 2:n + 2, gw:]
    f4_ref[:, n:n + 8, :] = zero_tail[:, :, gw:]
    f8_ref[:, 0:n, :] = f4_ref[:, 0:n, gw:] + f4_ref[:, 4:n + 4, gw:]
    f8_ref[:, n:n + 8, :] = zero_tail[:, :, 2 * gw:]
    sums = (
        f2_ref[:, HIST - 1:HIST - 1 + ta, 0:gw],
        f4_ref[:, HIST - 3:HIST - 3 + ta, 0:gw],
        f8_ref[:, HIST - 7:HIST - 7 + ta, 0:gw],
        f8_ref[:, HIST - 15:HIST - 15 + ta, gw:] + f8_ref[:, HIST - 7:HIST - 7 + ta, gw:],
    )
    t_idx = lax.broadcasted_iota(jnp.int32, (1, ta, 1), 1) + j * ta
    ys = []
    for gi, w in enumerate(POOL_WINDOWS):
        if prompt:
            cnt = jnp.minimum(t_idx + 1, w).astype(F32)
            mean = sums[gi] / cnt
        else:
            mean = sums[gi] * (1.0 / w)
        diff = mean - h[:, :, gi * gw:(gi + 1) * gw]
        ys.append(_dot(diff.reshape(nb * ta, gw).astype(BF16), wpool_ref[gi]))
    y = jnp.concatenate(ys, axis=-1) * scale_ref[...]
    x1 = x.reshape(nb * ta, d) + y
    x1_ref[...] = x1
    hf_ref[...] = _rms(x1, gffn_ref[...]).astype(BF16)
    if prompt:
        @pl.when(j == pl.num_programs(1) - 1)
        def _():
            state_ref[0] = e_ref[:, n - POOL_BUF:n, :]
    else:
        state_ref[0] = e_ref[:, n - POOL_BUF:n, :]


def _pool_front(x_p, x_s, state, gmix, gffn, wpool, scale, *, ta, nbs):
    bp, sp, d = x_p.shape
    bs, ss, _ = x_s.shape
    tp, ts = bp * sp, bs * ss
    gw = d // len(POOL_WINDOWS)
    assert sp % ta == 0 and ta % HIST == 0 and bs % nbs == 0
    vec = pl.BlockSpec((1, d), lambda *_: (0, 0))
    wspec = pl.BlockSpec((len(POOL_WINDOWS), gw, gw), lambda *_: (0, 0, 0))

    def scratch(nb, rows):
        return [pltpu.VMEM((nb, rows + HIST + 8, d), F32), pltpu.VMEM((nb, rows + HIST + 8, d), F32),
                pltpu.VMEM((nb, rows + HIST + 8, d - gw), F32), pltpu.VMEM((nb, rows + HIST + 8, d - 2 * gw), F32)]

    njp = sp // ta
    x1_p, hf_p, state_p = pl.pallas_call(
        functools.partial(_pool_front_kernel, prompt=True, nb=1, ta=ta, d=d),
        grid=(bp, njp),
        in_specs=[
            pl.BlockSpec((1, ta, d), lambda b, j: (b, j, 0)),
            pl.BlockSpec((1, HIST, d), lambda b, j: (b, jnp.maximum(j * (ta // HIST) - 1, 0), 0)),
            vec, vec, wspec, vec,
        ],
        out_specs=[
            pl.BlockSpec((ta, d), lambda b, j: (b * njp + j, 0)),
            pl.BlockSpec((ta, d), lambda b, j: (b * njp + j, 0)),
            pl.BlockSpec((1, 1, POOL_BUF, d), lambda b, j: (0, b, 0, 0)),
        ],
        out_shape=[jax.ShapeDtypeStruct((tp, d), F32), jax.ShapeDtypeStruct((tp, d), BF16),
                   jax.ShapeDtypeStruct((1, bp, POOL_BUF, d), F32)],
        scratch_shapes=scratch(1, ta),
        compiler_params=_params(2),
        name="pool_front_prompt",
    )(x_p, x_p, gmix, gffn, wpool, scale)

    rows = nbs * ss
    x1_s, hf_s, state_s = pl.pallas_call(
        functools.partial(_pool_front_kernel, prompt=False, nb=nbs, ta=ss, d=d),
        grid=(bs // nbs, 1),
        in_specs=[
            pl.BlockSpec((nbs, ss, d), lambda b, j: (b, 0, 0)),
            pl.BlockSpec((nbs, POOL_BUF, d), lambda b, j: (b, 0, 0)),
            vec, vec, wspec, vec,
        ],
        out_specs=[
            pl.BlockSpec((rows, d), lambda b, j: (b, 0)),
            pl.BlockSpec((rows, d), lambda b, j: (b, 0)),
            pl.BlockSpec((1, nbs, POOL_BUF, d), lambda b, j: (0, b, 0, 0)),
        ],
        out_shape=[jax.ShapeDtypeStruct((ts, d), F32), jax.ShapeDtypeStruct((ts, d), BF16),
                   jax.ShapeDtypeStruct((1, bs, POOL_BUF, d), F32)],
        scratch_shapes=scratch(nbs, ss),
        compiler_params=_params(2),
        name="pool_front_sample",
    )(x_s, state, gmix, gffn, wpool, scale)
    return x1_p, x1_s, hf_p, hf_s, state_p, state_s


def _swiglu_acc(x_bf, w1_ref, w3_ref, w2_ref, acc_ref, fc, after_chunk=None):
    d_ff = w1_ref.shape[-1]
    n_chunks = d_ff // fc
    for c in range(n_chunks):
        sl = slice(c * fc, (c + 1) * fc)
        h1 = _dot(x_bf, w1_ref[:, sl])
        h3 = _dot(x_bf, w3_ref[:, sl])
        a = (h1 * jax.nn.sigmoid(h1)) * h3
        acc_ref[...] += _dot(a.astype(BF16), w2_ref[sl, :])
        if after_chunk is not None:
            after_chunk(c, n_chunks)


def _ple_gate(x, p_bf, gple, wproj_ref, wgate_ref):
    e = _dot(p_bf, wproj_ref[...])
    gate = jax.nn.sigmoid(_dot(_rms(x, gple).astype(BF16), wgate_ref[...]))
    return x + gate * e


def _ffn0_post_kernel(hfp_ref, hfs_ref, x1p_ref, x1s_ref, pp_ref, ps_ref, w1_ref, w3_ref, w2_ref,
                      gple_ref, wproj_ref, wgate_ref, gkv_ref, wk_ref, bk_ref, wv_ref, bv_ref,
                      gmix_ref, wq_ref, bq_ref,
                      x3_ref, k_ref, v_ref, q_ref, acc_ref, *, fc, n_prompt_tiles):
    is_prompt = pl.program_id(0) < n_prompt_tiles
    acc_ref[...] = jnp.where(is_prompt, x1p_ref[...], x1s_ref[...])
    _swiglu_acc(jnp.where(is_prompt, hfp_ref[...], hfs_ref[...]), w1_ref, w3_ref, w2_ref, acc_ref, fc)
    p = jnp.where(is_prompt, pp_ref[...], ps_ref[...]).astype(BF16)
    x3 = _ple_gate(acc_ref[...], p, gple_ref[...], wproj_ref, wgate_ref)
    x3_ref[...] = x3
    kv_in = _rms(x3, gkv_ref[...]).astype(BF16)
    k_ref[...] = _dot(kv_in, wk_ref[...]) + bk_ref[...]
    v_ref[...] = _dot(kv_in, wv_ref[...]) + bv_ref[...]
    hq = _rms(x3, gmix_ref[...]).astype(BF16)
    q_ref[...] = (_dot(hq, wq_ref[...]) + bq_ref[...]).astype(BF16)


def _split_specs(tm, width, n_prompt_tiles):
    return [pl.BlockSpec((tm, width), lambda i, *_: (jnp.minimum(i, n_prompt_tiles - 1), 0)),
            pl.BlockSpec((tm, width), lambda i, *_: (jnp.maximum(i - n_prompt_tiles, 0), 0))]


def _ffn0_post(hf_p, hf_s, x1_p, x1_s, pp, ps, w1, w3, w2, gple, wproj, wgate, gkv, wk, bk, wv, bv, gmix, wq, bq,
               *, tm, fc):
    tp, d = x1_p.shape
    ts = x1_s.shape[0]
    t = tp + ts
    kvw = wk.shape[1]
    assert ts % tm == 0 and tp % tm == 0 and w1.shape[1] % fc == 0
    npt = tp // tm
    row = lambda w: pl.BlockSpec((tm, w), lambda i: (i, 0))
    vec = lambda w: pl.BlockSpec((1, w), lambda i: (0, 0))
    return pl.pallas_call(
        functools.partial(_ffn0_post_kernel, fc=fc, n_prompt_tiles=npt),
        grid=(t // tm,),
        in_specs=_split_specs(tm, d, npt) + _split_specs(tm, d, npt) + _split_specs(tm, pp.shape[1], npt) + [
            _const_spec(w1.shape), _const_spec(w3.shape), _const_spec(w2.shape),
            vec(d), _const_spec(wproj.shape), _const_spec(wgate.shape),
            vec(d), _const_spec(wk.shape), vec(kvw), _const_spec(wv.shape), vec(kvw),
            vec(d), _const_spec(wq.shape), vec(d),
        ],
        out_specs=[row(d), row(kvw), row(kvw), row(d)],
        out_shape=[jax.ShapeDtypeStruct((t, d), F32), jax.ShapeDtypeStruct((t, kvw), F32),
                   jax.ShapeDtypeStruct((t, kvw), F32), jax.ShapeDtypeStruct((t, d), BF16)],
        scratch_shapes=[pltpu.VMEM((tm, d), F32)],
        compiler_params=_params(1),
        name="ffn0_post",
    )(hf_p, hf_s, x1_p, x1_s, pp, ps, w1, w3, w2, gple, wproj, wgate, gkv, wk, bk, wv, bv, gmix, wq, bq)


def _swa_kernel(*refs, prompt, tq, n_kv, group, hd):
    if prompt:
        sink_ref, q_ref, ka_ref, kb_ref, va_ref, vb_ref, o_ref = refs
        kk = jnp.concatenate([ka_ref[0], kb_ref[0]], axis=0).astype(BF16)
        vv = jnp.concatenate([va_ref[0], vb_ref[0]], axis=0).astype(BF16)
    else:
        sink_ref, q_ref, k_ref, v_ref, o_ref = refs
        kk = k_ref[0].astype(BF16)
        vv = v_ref[0].astype(BF16)
    nk = kk.shape[0]
    q = q_ref[...]
    if prompt:
        j = pl.program_id(1)
        qc = (lax.broadcasted_iota(jnp.int32, (group * tq, nk), 0) % tq) // CHUNK
        kc = lax.broadcasted_iota(jnp.int32, (group * tq, nk), 1) // CHUNK
        lo = jnp.maximum(qc, N_BACK - j * (tq // CHUNK))
        bias = jnp.where(kc >= lo, jnp.where(kc <= qc + N_BACK, 0.0, NEG), NEG)
    outs = []
    for kvh in range(n_kv):
        kh = kk[:, kvh * hd:(kvh + 1) * hd]
        vh = vv[:, kvh * hd:(kvh + 1) * hd]
        heads = [kvh * group + gi for gi in range(group)]
        qs = jnp.concatenate([q[:, hh * hd:(hh + 1) * hd] for hh in heads], axis=0)
        s = lax.dot_general(qs, kh, (((1,), (1,)), ((), ())), preferred_element_type=F32) * (hd ** -0.5)
        if prompt:
            s = jnp.where(bias < 0.0, NEG, s)
        sink = jnp.concatenate([jnp.full((tq, 1), sink_ref[hh], F32) for hh in heads], axis=0)
        m = jnp.maximum(jnp.max(s, axis=-1, keepdims=True), sink)
        p = jnp.exp(s - m)
        den = jnp.sum(p, axis=-1, keepdims=True) + jnp.exp(sink - m)
        o = _dot(p.astype(BF16), vh) / den
        outs += [o[gi * tq:(gi + 1) * tq] for gi in range(group)]
    lanes = 128
    per = lanes // hd
    for c in range(len(outs) // per):
        o_ref[:, c * lanes:(c + 1) * lanes] = jnp.concatenate(outs[c * per:(c + 1) * per], axis=-1).astype(BF16)


def _swa(q, k, v, cache_k, cache_v, sinks, *, bp, sp, bs, ss, tq):
    t, d = q.shape
    kvw = k.shape[1]
    tp = bp * sp
    n_heads = sinks.shape[0]
    hd = d // n_heads
    n_kv = kvw // hd
    back = N_BACK * CHUNK
    assert tq == back and sp % tq == 0 and ss == CHUNK and cache_k.shape[1] == back
    kp = jnp.pad(k[:tp].reshape(bp, sp, kvw), ((0, 0), (back, 0), (0, 0)))
    vp = jnp.pad(v[:tp].reshape(bp, sp, kvw), ((0, 0), (back, 0), (0, 0)))
    ks = jnp.concatenate([cache_k.reshape(bs, back, kvw), k[tp:].reshape(bs, ss, kvw)], axis=1)
    vs = jnp.concatenate([cache_v.reshape(bs, back, kvw), v[tp:].reshape(bs, ss, kvw)], axis=1)
    smem = pl.BlockSpec(memory_space=pltpu.SMEM)
    nj = sp // tq
    kw = dict(n_kv=n_kv, group=n_heads // n_kv, hd=hd)
    o_p = pl.pallas_call(
        functools.partial(_swa_kernel, prompt=True, tq=tq, **kw),
        grid=(bp, nj),
        in_specs=[smem, pl.BlockSpec((tq, d), lambda b, j: (b * nj + j, 0)),
                  pl.BlockSpec((1, tq, kvw), lambda b, j: (b, j, 0)),
                  pl.BlockSpec((1, tq, kvw), lambda b, j: (b, j + 1, 0)),
                  pl.BlockSpec((1, tq, kvw), lambda b, j: (b, j, 0)),
                  pl.BlockSpec((1, tq, kvw), lambda b, j: (b, j + 1, 0))],
        out_specs=pl.BlockSpec((tq, d), lambda b, j: (b * nj + j, 0)),
        out_shape=jax.ShapeDtypeStruct((tp, d), BF16),
        compiler_params=_params(2),
        name="swa_prompt",
    )(sinks, q, kp, kp, vp, vp)
    off = tp // ss
    o_s = pl.pallas_call(
        functools.partial(_swa_kernel, prompt=False, tq=ss, **kw),
        grid=(bs,),
        in_specs=[smem, pl.BlockSpec((ss, d), lambda b: (off + b, 0)),
                  pl.BlockSpec((1, back + ss, kvw), lambda b: (b, 0, 0)),
                  pl.BlockSpec((1, back + ss, kvw), lambda b: (b, 0, 0))],
        out_specs=pl.BlockSpec((ss, d), lambda b: (b, 0)),
        out_shape=jax.ShapeDtypeStruct((bs * ss, d), BF16),
        compiler_params=_params(1),
        name="swa_sample",
    )(sinks, q, ks, vs)
    return o_p, o_s, kp, vp, ks, vs


def _oproj_route_kernel(op_ref, os_ref, x3_ref, wo_ref, gffn_ref, wr_ref, br_ref, x4_ref, hm_ref, route_ref,
                        *, n_prompt_tiles):
    o = jnp.where(pl.program_id(0) < n_prompt_tiles, op_ref[...], os_ref[...])
    x4 = x3_ref[...] + _dot(o, wo_ref[...])
    x4_ref[...] = x4
    hm = _rms(x4, gffn_ref[...])
    hm_ref[...] = hm
    logits = jnp.dot(hm, wr_ref[...], preferred_element_type=F32, precision=lax.Precision.HIGHEST) + br_ref[...]
    lane = lax.broadcasted_iota(jnp.int32, logits.shape, 1).astype(F32)
    m1 = jnp.max(logits, axis=-1, keepdims=True)
    i1 = jnp.min(jnp.where(logits == m1, lane, float(ROUTE_LANES)), axis=-1, keepdims=True)
    rest = jnp.where(lane == i1, -jnp.inf, logits)
    m2 = jnp.max(rest, axis=-1, keepdims=True)
    i2 = jnp.min(jnp.where(rest == m2, lane, float(ROUTE_LANES)), axis=-1, keepdims=True)
    e2 = jnp.exp(m2 - m1)
    g1 = 1.0 / (1.0 + e2)
    g2 = e2 / (1.0 + e2)
    route = jnp.where(lane == 0.0, i1, jnp.where(lane == 1.0, i2,
                      jnp.where(lane == 2.0, g1, jnp.where(lane == 3.0, g2, 0.0))))
    route_ref[...] = route


def _oproj_route(o_p, o_s, x3, wo, gffn, wr, br, *, tm):
    t, d = x3.shape
    tp = o_p.shape[0]
    assert tp % tm == 0 and o_s.shape[0] % tm == 0
    row = lambda w: pl.BlockSpec((tm, w), lambda i: (i, 0))
    return pl.pallas_call(
        functools.partial(_oproj_route_kernel, n_prompt_tiles=tp // tm),
        grid=(t // tm,),
        in_specs=_split_specs(tm, d, tp // tm) + [row(d), _const_spec(wo.shape), pl.BlockSpec((1, d), lambda i: (0, 0)),
                  _const_spec(wr.shape), pl.BlockSpec((1, ROUTE_LANES), lambda i: (0, 0))],
        out_specs=[row(d), row(d), row(ROUTE_LANES)],
        out_shape=[jax.ShapeDtypeStruct((t, d), F32), jax.ShapeDtypeStruct((t, d), F32),
                   jax.ShapeDtypeStruct((t, ROUTE_LANES), F32)],
        compiler_params=_params(1),
        name="oproj_route",
    )(o_p, o_s, x3, wo, gffn, wr, br)


def _moe_kernel(texp_ref, rtok_ref, rdst_ref, hm_ref, w1_ref, w3_ref, w2_ref, y_ref,
                xbuf0, xbuf1, obuf0, obuf1, acc_ref, gsem, ssem, *, tm, fc, n_steps):
    i = pl.program_id(0)

    def gather_row(base, r, xb, sem):
        tok = rtok_ref[base + r]
        pltpu.make_async_copy(hm_ref.at[pl.ds(tok, 1), :], xb.at[pl.ds(r, 1), :], sem).start()

    def scatter_row(base, r, ob, sem):
        dst = rdst_ref[base + r]
        pltpu.make_async_copy(ob.at[pl.ds(r, 1), :], y_ref.at[pl.ds(dst, 1), :], sem).start()

    def wait_gather(xb, sem):
        pltpu.make_async_copy(hm_ref.at[pl.ds(0, tm), :], xb, sem).wait()

    def wait_scatter(ob, sem):
        pltpu.make_async_copy(ob, y_ref.at[pl.ds(0, tm), :], sem).wait()

    @pl.when(i == 0)
    def _():
        obuf0[...] = jnp.zeros(obuf0.shape, F32)
        obuf1[...] = jnp.zeros(obuf1.shape, F32)
        lax.fori_loop(0, tm, lambda r, c: (gather_row(0, r, xbuf0, gsem.at[0]), c)[1], 0)

    def step(xb, ob, xb_next, ob_prev, p):
        wait_gather(xb, gsem.at[p])
        acc_ref[...] = jnp.zeros(acc_ref.shape, F32)
        g_base = (i + 1) * tm
        s_base = i * tm

        def move_rows(c, n_chunks):
            per = -(-tm // n_chunks)
            for r in range(c * per, min((c + 1) * per, tm)):
                gather_row(g_base, r, xb_next, gsem.at[1 - p])
                scatter_row(s_base, r, ob_prev, ssem.at[1 - p])

        _swiglu_acc(xb[...].astype(BF16), w1_ref, w3_ref, w2_ref, acc_ref, fc, after_chunk=move_rows)

        @pl.when(i >= 1)
        def _():
            wait_scatter(ob, ssem.at[p])
        ob[...] = acc_ref[...]

    @pl.when(i % 2 == 0)
    def _():
        step(xbuf0, obuf0, xbuf1, obuf1, 0)

    @pl.when(i % 2 == 1)
    def _():
        step(xbuf1, obuf1, xbuf0, obuf0, 1)

    @pl.when(i == n_steps - 1)
    def _():
        p = (n_steps - 1) % 2
        xb_next, ob_prev = (xbuf1, obuf1) if p == 0 else (xbuf0, obuf0)
        wait_gather(xb_next, gsem.at[1 - p])
        wait_scatter(ob_prev, ssem.at[1 - p])


def _moe(hm, route, w1, w3, w2, *, tm, fc):
    t, d = hm.shape
    n_exp = w1.shape[0]
    n_assign = t * TOP_K
    nt = n_assign // tm + n_exp
    assert n_assign % tm == 0
    e_flat = route[:, :TOP_K].astype(jnp.int32).reshape(-1)
    order = jnp.argsort(e_flat, stable=True).astype(jnp.int32)
    counts = jnp.sum((e_flat[:, None] == jnp.arange(n_exp, dtype=jnp.int32)[None, :]).astype(jnp.int32), axis=0)
    tiles_e = (counts + tm - 1) // tm
    tile_end = jnp.cumsum(tiles_e)
    tile_start = tile_end - tiles_e
    cnt_start = jnp.cumsum(counts) - counts
    tile_ids = jnp.arange(nt, dtype=jnp.int32)
    texp = jnp.minimum(jnp.searchsorted(tile_end, tile_ids, side="right"), n_exp - 1).astype(jnp.int32)
    within = (tile_ids - tile_start[texp]) * tm
    nval = jnp.where(tile_ids < tile_end[-1], jnp.clip(counts[texp] - within, 0, tm), 0).astype(jnp.int32)
    r_in = jnp.arange(tm, dtype=jnp.int32)[None, :]
    src = jnp.clip(cnt_start[texp][:, None] + within[:, None] + r_in, 0, n_assign - 1)
    real = r_in < nval[:, None]
    assign = jnp.where(real, order[src], 0)
    scratch_row = TOP_K * t + (tile_ids[:, None] % 2) * tm + r_in
    rtok = jnp.concatenate([(assign // TOP_K).reshape(-1), jnp.zeros((2 * tm,), jnp.int32)])
    rdst = jnp.concatenate([TOP_K * t + tm + r_in[0],
                            jnp.where(real, (assign % TOP_K) * t + assign // TOP_K, scratch_row).reshape(-1)])
    texp = jnp.concatenate([texp, texp[-1:]])
    n_steps = nt + 1

    wspec = lambda shape: pl.BlockSpec((None,) + shape[1:], lambda i, te, *_: (te[i], 0, 0),
                                       pipeline_mode=pl.Buffered(1))
    y2 = pl.pallas_call(
        functools.partial(_moe_kernel, tm=tm, fc=fc, n_steps=n_steps),
        grid_spec=pltpu.PrefetchScalarGridSpec(
            num_scalar_prefetch=3,
            grid=(n_steps,),
            in_specs=[pl.BlockSpec(memory_space=pl.ANY), wspec(w1.shape), wspec(w3.shape), wspec(w2.shape)],
            out_specs=pl.BlockSpec(memory_space=pl.ANY),
            scratch_shapes=[pltpu.VMEM((tm, d), F32)] * 5
            + [pltpu.SemaphoreType.DMA((2,)), pltpu.SemaphoreType.DMA((2,))],
        ),
        out_shape=jax.ShapeDtypeStruct((TOP_K * t + 2 * tm, d), F32),
        compiler_params=_params(1),
        name="moe",
    )(texp, rtok, rdst, hm, w1, w3, w2)
    return y2


def _final_kernel(x4_ref, ya_ref, yb_ref, route_ref, pp_ref, ps_ref, gple_ref, wproj_ref, wgate_ref, gfin_ref,
                  yp_ref, ys_ref, *, n_prompt_tiles):
    i = pl.program_id(0)
    route = route_ref[...]
    g1 = route[:, 2:3]
    g2 = route[:, 3:4]
    x5 = x4_ref[...] + (g1 * ya_ref[...] + g2 * yb_ref[...])
    p = jnp.where(i < n_prompt_tiles, pp_ref[...], ps_ref[...]).astype(BF16)
    x6 = _ple_gate(x5, p, gple_ref[...], wproj_ref, wgate_ref)
    y = _rms(x6, gfin_ref[...])

    @pl.when(i < n_prompt_tiles)
    def _():
        yp_ref[...] = y

    @pl.when(i >= n_prompt_tiles)
    def _():
        ys_ref[...] = y


def _final(x4, y2, route, pp, ps, gple, wproj, wgate, gfin, *, tm):
    t, d = x4.shape
    tp, ts = pp.shape[0], ps.shape[0]
    npt = tp // tm
    assert tp % tm == 0 and ts % tm == 0
    row = lambda w: pl.BlockSpec((tm, w), lambda i: (i, 0))
    vec = pl.BlockSpec((1, d), lambda i: (0, 0))
    return pl.pallas_call(
        functools.partial(_final_kernel, n_prompt_tiles=npt),
        grid=(t // tm,),
        in_specs=[row(d), row(d), pl.BlockSpec((tm, d), lambda i: (t // tm + i, 0)), row(ROUTE_LANES)]
        + _split_specs(tm, pp.shape[1], npt) + [vec, _const_spec(wproj.shape), _const_spec(wgate.shape), vec],
        out_specs=_split_specs(tm, d, npt),
        out_shape=[jax.ShapeDtypeStruct((tp, d), F32), jax.ShapeDtypeStruct((ts, d), F32)],
        compiler_params=_params(1),
        name="final",
    )(x4, y2, y2, route, pp, ps, gple, wproj, wgate, gfin)


def kernel(x_prompt, x_sample, state_pool, cache_k, cache_v, p_prompt, p_sample, g_mix, g_ffn, g_ple, g_kv, g_final,
           w_pool, pool_scale, w_q, b_q, w_k, b_k, w_v, b_v, w_o, sinks, w1_dense, w3_dense, w2_dense, w_router,
           b_router, w1_moe, w3_moe, w2_moe, w_ple_proj, w_ple_gate):
    bp, sp, d = x_prompt.shape
    bs, ss, _ = x_sample.shape
    depth = g_mix.shape[0]
    assert depth == 2 and w_pool.shape[0] == 1 and w_q.shape[0] == 1 and w1_dense.shape[0] == 1 and w1_moe.shape[0] == 1
    tp, ts = bp * sp, bs * ss
    n_kv, hd = cache_k.shape[2], cache_k.shape[3]
    n_exp = w_router.shape[2]
    ple = p_prompt.shape[-1]
    bf = lambda w: w.astype(BF16)
    vec = lambda g: g.reshape(1, -1)

    x1_p, x1_s, hf_p, hf_s, pool_p, pool_s = _pool_front(
        x_prompt, x_sample, state_pool[0], vec(g_mix[0]), vec(g_ffn[0]), bf(w_pool[0]), vec(pool_scale[0]),
        ta=512, nbs=8)

    x3, k, v, q = _ffn0_post(
        hf_p, hf_s, x1_p, x1_s, p_prompt[0].reshape(tp, ple), p_sample[0].reshape(ts, ple),
        bf(w1_dense[0]), bf(w3_dense[0]), bf(w2_dense[0]),
        vec(g_ple[0]), bf(w_ple_proj[0]), bf(w_ple_gate[0]),
        vec(g_kv), bf(w_k), vec(b_k), bf(w_v), vec(b_v),
        vec(g_mix[1]), bf(w_q[0]), vec(b_q[0]), tm=512, fc=512)

    o_p, o_s, kp, vp, ks, vs = _swa(q, k, v, cache_k, cache_v, sinks[0], bp=bp, sp=sp, bs=bs, ss=ss,
                                    tq=N_BACK * CHUNK)

    wr = jnp.pad(w_router[0], ((0, 0), (0, ROUTE_LANES - n_exp)))
    br = jnp.pad(b_router[0], (0, ROUTE_LANES - n_exp), constant_values=NEG).reshape(1, ROUTE_LANES)
    x4, hm, route = _oproj_route(o_p, o_s, x3, bf(w_o[0]), vec(g_ffn[1]), wr, br, tm=512)

    y2 = _moe(hm, route, bf(w1_moe[0]), bf(w3_moe[0]), bf(w2_moe[0]), tm=256, fc=512)

    y_p, y_s = _final(x4, y2, route, p_prompt[1].reshape(tp, ple), p_sample[1].reshape(ts, ple),
                      vec(g_ple[1]), bf(w_ple_proj[1]), bf(w_ple_gate[1]), vec(g_final), tm=512)

    back = N_BACK * CHUNK
    return (y_p.reshape(bp, sp, d), y_s.reshape(bs, ss, d), pool_p, pool_s,
            kp[:, -back:].reshape(bp, back, n_kv, hd), vp[:, -back:].reshape(bp, back, n_kv, hd),
            ks[:, -back:].reshape(bs, back, n_kv, hd), vs[:, -back:].reshape(bs, back, n_kv, hd))
```

```python
import functools

import jax
import jax.numpy as jnp
from jax import lax
from jax.experimental import pallas as pl
from jax.experimental.pallas import tpu as pltpu

EPS = 1e-6
NEG = -1e30
CHUNK = 64
N_BACK = 2
POOL_WINDOWS = (2, 4, 8, 16)
POOL_BUF = 15
HIST = 16
TOP_K = 2
ROUTE_LANES = 128
VMEM_LIMIT = 56 * 1024 * 1024

F32 = jnp.float32
BF16 = jnp.bfloat16


def _rms(x, g):
    return (x * lax.rsqrt(jnp.mean(x * x, axis=-1, keepdims=True) + EPS)) * g


def _dot(a, b):
    return jnp.dot(a, b, preferred_element_type=F32)


def _const_spec(shape):
    nd = len(shape)
    return pl.BlockSpec(shape, lambda *_: (0,) * nd, pipeline_mode=pl.Buffered(1))


def _params(n_axes):
    return pltpu.CompilerParams(dimension_semantics=("arbitrary",) * n_axes, vmem_limit_bytes=VMEM_LIMIT)


def _pool_front_kernel(*refs, prompt, nb, ta, d):
    (x_ref, hist_ref, gmix_ref, gffn_ref, wpool_ref, scale_ref,
     x1_ref, hf_ref, state_ref, e_ref, f2_ref, f4_ref, f8_ref) = refs
    gw = d // len(POOL_WINDOWS)
    j = pl.program_id(1)
    n = ta + HIST
    x = x_ref[...]
    g = gmix_ref[...]
    h = _rms(x, g)
    if prompt:
        hp = _rms(hist_ref[...], g)
        hp = jnp.where(j == 0, 0.0, hp)
        e_ref[:, 0:HIST, :] = hp
    else:
        e_ref[:, 0:1, :] = jnp.zeros((nb, 1, d), F32)
        e_ref[:, 1:HIST, :] = hist_ref[...]
    e_ref[:, HIST:n, :] = h
    zero_tail = jnp.zeros((nb, 8, d), F32)
    e_ref[:, n:n + 8, :] = zero_tail
    f2_ref[:, 0:n, :] = e_ref[:, 0:n, :] + e_ref[:, 1:n + 1, :]
    f2_ref[:, n:n + 8, :] = zero_tail
    f4_ref[:, 0:n, :] = f2_ref[:, 0:n, gw:] + f2_ref[:, 2:n + 2, gw:]
    f4_ref[:, n:n + 8, :] = zero_tail[:, :, gw:]
    f8_ref[:, 0:n, :] = f4_ref[:, 0:n, gw:] + f4_ref[:, 4:n + 4, gw:]
    f8_ref[:, n:n + 8, :] = zero_tail[:, :, 2 * gw:]
    sums = (
        f2_ref[:, HIST - 1:HIST - 1 + ta, 0:gw],
        f4_ref[:, HIST - 3:HIST - 3 + ta, 0:gw],
        f8_ref[:, HIST - 7:HIST - 7 + ta, 0:gw],
        f8_ref[:, HIST - 15:HIST - 15 + ta, gw:] + f8_ref[:, HIST - 7:HIST - 7 + ta, gw:],
    )
    t_idx = lax.broadcasted_iota(jnp.int32, (1, ta, 1), 1) + j * ta
    ys = []
    for gi, w in enumerate(POOL_WINDOWS):
        if prompt:
            cnt = jnp.minimum(t_idx + 1, w).astype(F32)
            mean = sums[gi] / cnt
        else:
            mean = sums[gi] * (1.0 / w)
        diff = mean - h[:, :, gi * gw:(gi + 1) * gw]
        ys.append(_dot(diff.reshape(nb * ta, gw).astype(BF16), wpool_ref[gi]))
    y = jnp.concatenate(ys, axis=-1) * scale_ref[...]
    x1 = x.reshape(nb * ta, d) + y
    x1_ref[...] = x1
    hf_ref[...] = _rms(x1, gffn_ref[...]).astype(BF16)
    if prompt:
        @pl.when(j == pl.num_programs(1) - 1)
        def _():
            state_ref[0] = e_ref[:, n - POOL_BUF:n, :]
    else:
        state_ref[0] = e_ref[:, n - POOL_BUF:n, :]


def _pool_front(x_p, x_s, state, gmix, gffn, wpool, scale, *, ta, nbs):
    bp, sp, d = x_p.shape
    bs, ss, _ = x_s.shape
    tp, ts = bp * sp, bs * ss
    gw = d // len(POOL_WINDOWS)
    assert sp % ta == 0 and ta % HIST == 0 and bs % nbs == 0
    vec = pl.BlockSpec((1, d), lambda *_: (0, 0))
    wspec = pl.BlockSpec((len(POOL_WINDOWS), gw, gw), lambda *_: (0, 0, 0))

    def scratch(nb, rows):
        return [pltpu.VMEM((nb, rows + HIST + 8, d), F32), pltpu.VMEM((nb, rows + HIST + 8, d), F32),
                pltpu.VMEM((nb, rows + HIST + 8, d - gw), F32), pltpu.VMEM((nb, rows + HIST + 8, d - 2 * gw), F32)]

    njp = sp // ta
    x1_p, hf_p, state_p = pl.pallas_call(
        functools.partial(_pool_front_kernel, prompt=True, nb=1, ta=ta, d=d),
        grid=(bp, njp),
        in_specs=[
            pl.BlockSpec((1, ta, d), lambda b, j: (b, j, 0)),
            pl.BlockSpec((1, HIST, d), lambda b, j: (b, jnp.maximum(j * (ta // HIST) - 1, 0), 0)),
            vec, vec, wspec, vec,
        ],
        out_specs=[
            pl.BlockSpec((ta, d), lambda b, j: (b * njp + j, 0)),
            pl.BlockSpec((ta, d), lambda b, j: (b * njp + j, 0)),
            pl.BlockSpec((1, 1, POOL_BUF, d), lambda b, j: (0, b, 0, 0)),
        ],
        out_shape=[jax.ShapeDtypeStruct((tp, d), F32), jax.ShapeDtypeStruct((tp, d), BF16),
                   jax.ShapeDtypeStruct((1, bp, POOL_BUF, d), F32)],
        scratch_shapes=scratch(1, ta),
        compiler_params=_params(2),
        name="pool_front_prompt",
    )(x_p, x_p, gmix, gffn, wpool, scale)

    rows = nbs * ss
    x1_s, hf_s, state_s = pl.pallas_call(
        functools.partial(_pool_front_kernel, prompt=False, nb=nbs, ta=ss, d=d),
        grid=(bs // nbs, 1),
        in_specs=[
            pl.BlockSpec((nbs, ss, d), lambda b, j: (b, 0, 0)),
            pl.BlockSpec((nbs, POOL_BUF, d), lambda b, j: (b, 0, 0)),
            vec, vec, wspec, vec,
        ],
        out_specs=[
            pl.BlockSpec((rows, d), lambda b, j: (b, 0)),
            pl.BlockSpec((rows, d), lambda b, j: (b, 0)),
            pl.BlockSpec((1, nbs, POOL_BUF, d), lambda b, j: (0, b, 0, 0)),
        ],
        out_shape=[jax.ShapeDtypeStruct((ts, d), F32), jax.ShapeDtypeStruct((ts, d), BF16),
                   jax.ShapeDtypeStruct((1, bs, POOL_BUF, d), F32)],
        scratch_shapes=scratch(nbs, ss),
        compiler_params=_params(2),
        name="pool_front_sample",
    )(x_s, state, gmix, gffn, wpool, scale)
    return x1_p, x1_s, hf_p, hf_s, state_p, state_s


def _swiglu_acc(x_bf, w1_ref, w3_ref, w2_ref, acc_ref, fc, after_chunk=None):
    d_ff = w1_ref.shape[-1]
    n_chunks = d_ff // fc
    for c in range(n_chunks):
        sl = slice(c * fc, (c + 1) * fc)
        h1 = _dot(x_bf, w1_ref[:, sl])
        h3 = _dot(x_bf, w3_ref[:, sl])
        a = (h1 * jax.nn.sigmoid(h1)) * h3
        acc_ref[...] += _dot(a.astype(BF16), w2_ref[sl, :])
        if after_chunk is not None:
            after_chunk(c, n_chunks)


def _ple_gate(x, p_bf, gple, wproj_ref, wgate_ref):
    e = _dot(p_bf, wproj_ref[...])
    gate = jax.nn.sigmoid(_dot(_rms(x, gple).astype(BF16), wgate_ref[...]))
    return x + gate * e


def _ffn0_post_kernel(hfp_ref, hfs_ref, x1p_ref, x1s_ref, pp_ref, ps_ref, w1_ref, w3_ref, w2_ref,
                      gple_ref, wproj_ref, wgate_ref, gkv_ref, wk_ref, bk_ref, wv_ref, bv_ref,
                      gmix_ref, wq_ref, bq_ref,
                      x3_ref, k_ref, v_ref, q_ref, acc_ref, *, fc, n_prompt_tiles):
    is_prompt = pl.program_id(0) < n_prompt_tiles
    acc_ref[...] = jnp.where(is_prompt, x1p_ref[...], x1s_ref[...])
    _swiglu_acc(jnp.where(is_prompt, hfp_ref[...], hfs_ref[...]), w1_ref, w3_ref, w2_ref, acc_ref, fc)
    p = jnp.where(is_prompt, pp_ref[...], ps_ref[...]).astype(BF16)
    x3 = _ple_gate(acc_ref[...], p, gple_ref[...], wproj_ref, wgate_ref)
    x3_ref[...] = x3
    kv_in = _rms(x3, gkv_ref[...]).astype(BF16)
    k_ref[...] = _dot(kv_in, wk_ref[...]) + bk_ref[...]
    v_ref[...] = _dot(kv_in, wv_ref[...]) + bv_ref[...]
    hq = _rms(x3, gmix_ref[...]).astype(BF16)
    q_ref[...] = (_dot(hq, wq_ref[...]) + bq_ref[...]).astype(BF16)


def _split_specs(tm, width, n_prompt_tiles):
    return [pl.BlockSpec((tm, width), lambda i, *_: (jnp.minimum(i, n_prompt_tiles - 1), 0)),
            pl.BlockSpec((tm, width), lambda i, *_: (jnp.maximum(i - n_prompt_tiles, 0), 0))]


def _ffn0_post(hf_p, hf_s, x1_p, x1_s, pp, ps, w1, w3, w2, gple, wproj, wgate, gkv, wk, bk, wv, bv, gmix, wq, bq,
               *, tm, fc):
    tp, d = x1_p.shape
    ts = x1_s.shape[0]
    t = tp + ts
    kvw = wk.shape[1]
    assert ts % tm == 0 and tp % tm == 0 and w1.shape[1] % fc == 0
    npt = tp // tm
    row = lambda w: pl.BlockSpec((tm, w), lambda i: (i, 0))
    vec = lambda w: pl.BlockSpec((1, w), lambda i: (0, 0))
    return pl.pallas_call(
        functools.partial(_ffn0_post_kernel, fc=fc, n_prompt_tiles=npt),
        grid=(t // tm,),
        in_specs=_split_specs(tm, d, npt) + _split_specs(tm, d, npt) + _split_specs(tm, pp.shape[1], npt) + [
            _const_spec(w1.shape), _const_spec(w3.shape), _const_spec(w2.shape),
            vec(d), _const_spec(wproj.shape), _const_spec(wgate.shape),
            vec(d), _const_spec(wk.shape), vec(kvw), _const_spec(wv.shape), vec(kvw),
            vec(d), _const_spec(wq.shape), vec(d),
        ],
        out_specs=[row(d), row(kvw), row(kvw), row(d)],
        out_shape=[jax.ShapeDtypeStruct((t, d), F32), jax.ShapeDtypeStruct((t, kvw), F32),
                   jax.ShapeDtypeStruct((t, kvw), F32), jax.ShapeDtypeStruct((t, d), BF16)],
        scratch_shapes=[pltpu.VMEM((tm, d), F32)],
        compiler_params=_params(1),
        name="ffn0_post",
    )(hf_p, hf_s, x1_p, x1_s, pp, ps, w1, w3, w2, gple, wproj, wgate, gkv, wk, bk, wv, bv, gmix, wq, bq)


def _swa_kernel(*refs, prompt, tq, n_kv, group, hd):
    if prompt:
        sink_ref, q_ref, ka_ref, kb_ref, va_ref, vb_ref, o_ref = refs
        kk = jnp.concatenate([ka_ref[0], kb_ref[0]], axis=0)
        vv = jnp.concatenate([va_ref[0], vb_ref[0]], axis=0)
        first_chunk = N_BACK - pl.program_id(1) * (tq // CHUNK)
    else:
        sink_ref, q_ref, k_ref, v_ref, o_ref = refs
        pad = jnp.zeros((CHUNK, k_ref.shape[2]), F32)
        kk = jnp.concatenate([k_ref[0], pad], axis=0)
        vv = jnp.concatenate([v_ref[0], pad], axis=0)
        first_chunk = 0
    nk = kk.shape[0]
    lanes = 128
    per = lanes // hd
    assert per == 2 and group % per == 0 and nk % lanes == 0
    qc = lax.broadcasted_iota(jnp.int32, (tq, per * nk), 0) // CHUNK
    kc = (lax.broadcasted_iota(jnp.int32, (tq, per * nk), 1) % nk) // CHUNK
    bias = jnp.where(kc >= jnp.maximum(qc, first_chunk), jnp.where(kc <= qc + N_BACK, 0.0, NEG), NEG)
    lane = lax.broadcasted_iota(jnp.int32, (nk, lanes), 1)
    ones_a = jnp.where(lane < hd, 1.0, 0.0)
    ones_b = jnp.where(lane >= hd, 1.0, 0.0)
    out_lane = lax.broadcasted_iota(jnp.int32, (tq, lanes), 1)
    for kvh in range(n_kv):
        tile, half = divmod(kvh, per)
        keep = (lane < hd) if half == 0 else (lane >= hd)

        def both_halves(x):
            own = jnp.where(keep, x[:, tile * lanes:(tile + 1) * lanes], 0.0)
            other = pltpu.roll(own, hd, axis=1)
            return (own, other) if half == 0 else (other, own)

        k_a, k_b = both_halves(kk * (hd ** -0.5))
        v_a, v_b = both_halves(vv)
        kbd = jnp.concatenate([k_a, k_b], axis=0).astype(BF16)
        vobd = jnp.concatenate([jnp.concatenate([v_a, ones_a], axis=1),
                                jnp.concatenate([v_b, ones_b], axis=1)], axis=0).astype(BF16)
        for pr in range(group // per):
            head_a = kvh * group + pr * per
            col0 = head_a * hd
            s = lax.dot_general(q_ref[:, col0:col0 + lanes], kbd, (((1,), (1,)), ((), ())),
                                preferred_element_type=F32) + bias
            sink_a, sink_b = sink_ref[head_a], sink_ref[head_a + 1]
            m_a = jnp.maximum(jnp.max(s[:, :nk], axis=-1, keepdims=True), sink_a)
            m_b = jnp.maximum(jnp.max(s[:, nk:], axis=-1, keepdims=True), sink_b)
            p = jnp.concatenate([jnp.exp(s[:, :nk] - m_a), jnp.exp(s[:, nk:] - m_b)], axis=1).astype(BF16)
            r = _dot(p, vobd)
            sink_term = jnp.where(out_lane < hd, jnp.exp(sink_a - m_a), jnp.exp(sink_b - m_b))
            o_ref[:, col0:col0 + lanes] = (r[:, :lanes] / (r[:, lanes:] + sink_term)).astype(BF16)


def _swa(q, k, v, cache_k, cache_v, sinks, *, bp, sp, bs, ss, tq):
    t, d = q.shape
    kvw = k.shape[1]
    tp = bp * sp
    n_heads = sinks.shape[0]
    hd = d // n_heads
    n_kv = kvw // hd
    back = N_BACK * CHUNK
    assert tq == back and sp % tq == 0 and ss == CHUNK and cache_k.shape[1] == back
    kp = jnp.pad(k[:tp].reshape(bp, sp, kvw), ((0, 0), (back, 0), (0, 0)))
    vp = jnp.pad(v[:tp].reshape(bp, sp, kvw), ((0, 0), (back, 0), (0, 0)))
    ks = jnp.concatenate([cache_k.reshape(bs, back, kvw), k[tp:].reshape(bs, ss, kvw)], axis=1)
    vs = jnp.concatenate([cache_v.reshape(bs, back, kvw), v[tp:].reshape(bs, ss, kvw)], axis=1)
    smem = pl.BlockSpec(memory_space=pltpu.SMEM)
    nj = sp // tq
    kw = dict(n_kv=n_kv, group=n_heads // n_kv, hd=hd)
    o_p = pl.pallas_call(
        functools.partial(_swa_kernel, prompt=True, tq=tq, **kw),
        grid=(bp, nj),
        in_specs=[smem, pl.BlockSpec((tq, d), lambda b, j: (b * nj + j, 0)),
                  pl.BlockSpec((1, tq, kvw), lambda b, j: (b, j, 0)),
                  pl.BlockSpec((1, tq, kvw), lambda b, j: (b, j + 1, 0)),
                  pl.BlockSpec((1, tq, kvw), lambda b, j: (b, j, 0)),
                  pl.BlockSpec((1, tq, kvw), lambda b, j: (b, j + 1, 0))],
        out_specs=pl.BlockSpec((tq, d), lambda b, j: (b * nj + j, 0)),
        out_shape=jax.ShapeDtypeStruct((tp, d), BF16),
        compiler_params=_params(2),
        name="swa_prompt",
    )(sinks, q, kp, kp, vp, vp)
    off = tp // ss
    o_s = pl.pallas_call(
        functools.partial(_swa_kernel, prompt=False, tq=ss, **kw),
        grid=(bs,),
        in_specs=[smem, pl.BlockSpec((ss, d), lambda b: (off + b, 0)),
                  pl.BlockSpec((1, back + ss, kvw), lambda b: (b, 0, 0)),
                  pl.BlockSpec((1, back + ss, kvw), lambda b: (b, 0, 0))],
        out_specs=pl.BlockSpec((ss, d), lambda b: (b, 0)),
        out_shape=jax.ShapeDtypeStruct((bs * ss, d), BF16),
        compiler_params=_params(1),
        name="swa_sample",
    )(sinks, q, ks, vs)
    return o_p, o_s, kp, vp, ks, vs


def _oproj_route_kernel(op_ref, os_ref, x3_ref, wo_ref, gffn_ref, wr_ref, br_ref, x4_ref, hm_ref, route_ref,
                        *, n_prompt_tiles):
    o = jnp.where(pl.program_id(0) < n_prompt_tiles, op_ref[...], os_ref[...])
    x4 = x3_ref[...] + _dot(o, wo_ref[...])
    x4_ref[...] = x4
    hm = _rms(x4, gffn_ref[...])
    hm_ref[...] = hm
    hm_hi = hm.astype(BF16)
    hm_lo = (hm - hm_hi.astype(F32)).astype(BF16)
    wr = wr_ref[...]
    wr_hi = wr.astype(BF16)
    wr_lo = (wr - wr_hi.astype(F32)).astype(BF16)
    logits = (_dot(hm_hi, wr_hi) + (_dot(hm_lo, wr_hi) + _dot(hm_hi, wr_lo))) + br_ref[...]
    lane = lax.broadcasted_iota(jnp.int32, logits.shape, 1).astype(F32)
    m1 = jnp.max(logits, axis=-1, keepdims=True)
    i1 = jnp.min(jnp.where(logits == m1, lane, float(ROUTE_LANES)), axis=-1, keepdims=True)
    rest = jnp.where(lane == i1, -jnp.inf, logits)
    m2 = jnp.max(rest, axis=-1, keepdims=True)
    i2 = jnp.min(jnp.where(rest == m2, lane, float(ROUTE_LANES)), axis=-1, keepdims=True)
    e2 = jnp.exp(m2 - m1)
    g1 = 1.0 / (1.0 + e2)
    g2 = e2 / (1.0 + e2)
    route = jnp.where(lane == 0.0, i1, jnp.where(lane == 1.0, i2,
                      jnp.where(lane == 2.0, g1, jnp.where(lane == 3.0, g2, 0.0))))
    route_ref[...] = route


def _oproj_route(o_p, o_s, x3, wo, gffn, wr, br, *, tm):
    t, d = x3.shape
    tp = o_p.shape[0]
    assert tp % tm == 0 and o_s.shape[0] % tm == 0
    row = lambda w: pl.BlockSpec((tm, w), lambda i: (i, 0))
    return pl.pallas_call(
        functools.partial(_oproj_route_kernel, n_prompt_tiles=tp // tm),
        grid=(t // tm,),
        in_specs=_split_specs(tm, d, tp // tm) + [row(d), _const_spec(wo.shape), pl.BlockSpec((1, d), lambda i: (0, 0)),
                  _const_spec(wr.shape), pl.BlockSpec((1, ROUTE_LANES), lambda i: (0, 0))],
        out_specs=[row(d), row(d), row(ROUTE_LANES)],
        out_shape=[jax.ShapeDtypeStruct((t, d), F32), jax.ShapeDtypeStruct((t, d), F32),
                   jax.ShapeDtypeStruct((t, ROUTE_LANES), F32)],
        compiler_params=_params(1),
        name="oproj_route",
    )(o_p, o_s, x3, wo, gffn, wr, br)


def _moe_kernel(texp_ref, rtok_ref, rdst_ref, hm_ref, w1_ref, w3_ref, w2_ref, y_ref,
                xbuf0, xbuf1, obuf0, obuf1, acc_ref, gsem, ssem, *, tm, fc, n_steps):
    i = pl.program_id(0)

    def gather_row(base, r, xb, sem):
        tok = rtok_ref[base + r]
        pltpu.make_async_copy(hm_ref.at[pl.ds(tok, 1), :], xb.at[pl.ds(r, 1), :], sem).start()

    def scatter_row(base, r, ob, sem):
        dst = rdst_ref[base + r]
        pltpu.make_async_copy(ob.at[pl.ds(r, 1), :], y_ref.at[pl.ds(dst, 1), :], sem).start()

    def wait_gather(xb, sem):
        pltpu.make_async_copy(hm_ref.at[pl.ds(0, tm), :], xb, sem).wait()

    def wait_scatter(ob, sem):
        pltpu.make_async_copy(ob, y_ref.at[pl.ds(0, tm), :], sem).wait()

    @pl.when(i == 0)
    def _():
        obuf0[...] = jnp.zeros(obuf0.shape, F32)
        obuf1[...] = jnp.zeros(obuf1.shape, F32)
        lax.fori_loop(0, tm, lambda r, c: (gather_row(0, r, xbuf0, gsem.at[0]), c)[1], 0)

    def step(xb, ob, xb_next, ob_prev, p):
        wait_gather(xb, gsem.at[p])
        acc_ref[...] = jnp.zeros(acc_ref.shape, F32)
        g_base = (i + 1) * tm
        s_base = i * tm

        def move_rows(c, n_chunks):
            per = -(-tm // n_chunks)
            for r in range(c * per, min((c + 1) * per, tm)):
                gather_row(g_base, r, xb_next, gsem.at[1 - p])
                scatter_row(s_base, r, ob_prev, ssem.at[1 - p])

        _swiglu_acc(xb[...].astype(BF16), w1_ref, w3_ref, w2_ref, acc_ref, fc, after_chunk=move_rows)

        @pl.when(i >= 1)
        def _():
            wait_scatter(ob, ssem.at[p])
        ob[...] = acc_ref[...]

    @pl.when(i % 2 == 0)
    def _():
        step(xbuf0, obuf0, xbuf1, obuf1, 0)

    @pl.when(i % 2 == 1)
    def _():
        step(xbuf1, obuf1, xbuf0, obuf0, 1)

    @pl.when(i == n_steps - 1)
    def _():
        p = (n_steps - 1) % 2
        xb_next, ob_prev = (xbuf1, obuf1) if p == 0 else (xbuf0, obuf0)
        wait_gather(xb_next, gsem.at[1 - p])
        wait_scatter(ob_prev, ssem.at[1 - p])


def _moe(hm, route, w1, w3, w2, *, tm, fc):
    t, d = hm.shape
    n_exp = w1.shape[0]
    n_assign = t * TOP_K
    nt = n_assign // tm + n_exp
    assert n_assign % tm == 0
    e_flat = route[:, :TOP_K].astype(jnp.int32).reshape(-1)
    order = jnp.argsort(e_flat, stable=True).astype(jnp.int32)
    counts = jnp.sum((e_flat[:, None] == jnp.arange(n_exp, dtype=jnp.int32)[None, :]).astype(jnp.int32), axis=0)
    tiles_e = (counts + tm - 1) // tm
    tile_end = jnp.cumsum(tiles_e)
    tile_start = tile_end - tiles_e
    cnt_start = jnp.cumsum(counts) - counts
    tile_ids = jnp.arange(nt, dtype=jnp.int32)
    texp = jnp.minimum(jnp.searchsorted(tile_end, tile_ids, side="right"), n_exp - 1).astype(jnp.int32)
    within = (tile_ids - tile_start[texp]) * tm
    nval = jnp.where(tile_ids < tile_end[-1], jnp.clip(counts[texp] - within, 0, tm), 0).astype(jnp.int32)
    r_in = jnp.arange(tm, dtype=jnp.int32)[None, :]
    src = jnp.clip(cnt_start[texp][:, None] + within[:, None] + r_in, 0, n_assign - 1)
    real = r_in < nval[:, None]
    assign = jnp.where(real, order[src], 0)
    scratch_row = TOP_K * t + (tile_ids[:, None] % 2) * tm + r_in
    rtok = jnp.concatenate([(assign // TOP_K).reshape(-1), jnp.zeros((2 * tm,), jnp.int32)])
    rdst = jnp.concatenate([TOP_K * t + tm + r_in[0],
                            jnp.where(real, (assign % TOP_K) * t + assign // TOP_K, scratch_row).reshape(-1)])
    texp = jnp.concatenate([texp, texp[-1:]])
    n_steps = nt + 1

    wspec = lambda shape: pl.BlockSpec((None,) + shape[1:], lambda i, te, *_: (te[i], 0, 0),
                                       pipeline_mode=pl.Buffered(1))
    y2 = pl.pallas_call(
        functools.partial(_moe_kernel, tm=tm, fc=fc, n_steps=n_steps),
        grid_spec=pltpu.PrefetchScalarGridSpec(
            num_scalar_prefetch=3,
            grid=(n_steps,),
            in_specs=[pl.BlockSpec(memory_space=pl.ANY), wspec(w1.shape), wspec(w3.shape), wspec(w2.shape)],
            out_specs=pl.BlockSpec(memory_space=pl.ANY),
            scratch_shapes=[pltpu.VMEM((tm, d), F32)] * 5
            + [pltpu.SemaphoreType.DMA((2,)), pltpu.SemaphoreType.DMA((2,))],
        ),
        out_shape=jax.ShapeDtypeStruct((TOP_K * t + 2 * tm, d), F32),
        compiler_params=_params(1),
        name="moe",
    )(texp, rtok, rdst, hm, w1, w3, w2)
    return y2


def _final_kernel(x4_ref, ya_ref, yb_ref, route_ref, pp_ref, ps_ref, gple_ref, wproj_ref, wgate_ref, gfin_ref,
                  yp_ref, ys_ref, *, n_prompt_tiles):
    i = pl.program_id(0)
    route = route_ref[...]
    g1 = route[:, 2:3]
    g2 = route[:, 3:4]
    x5 = x4_ref[...] + (g1 * ya_ref[...] + g2 * yb_ref[...])
    p = jnp.where(i < n_prompt_tiles, pp_ref[...], ps_ref[...]).astype(BF16)
    x6 = _ple_gate(x5, p, gple_ref[...], wproj_ref, wgate_ref)
    y = _rms(x6, gfin_ref[...])

    @pl.when(i < n_prompt_tiles)
    def _():
        yp_ref[...] = y

    @pl.when(i >= n_prompt_tiles)
    def _():
        ys_ref[...] = y


def _final(x4, y2, route, pp, ps, gple, wproj, wgate, gfin, *, tm):
    t, d = x4.shape
    tp, ts = pp.shape[0], ps.shape[0]
    npt = tp // tm
    assert tp % tm == 0 and ts % tm == 0
    row = lambda w: pl.BlockSpec((tm, w), lambda i: (i, 0))
    vec = pl.BlockSpec((1, d), lambda i: (0, 0))
    return pl.pallas_call(
        functools.partial(_final_kernel, n_prompt_tiles=npt),
        grid=(t // tm,),
        in_specs=[row(d), row(d), pl.BlockSpec((tm, d), lambda i: (t // tm + i, 0)), row(ROUTE_LANES)]
        + _split_specs(tm, pp.shape[1], npt) + [vec, _const_spec(wproj.shape), _const_spec(wgate.shape), vec],
        out_specs=_split_specs(tm, d, npt),
        out_shape=[jax.ShapeDtypeStruct((tp, d), F32), jax.ShapeDtypeStruct((ts, d), F32)],
        compiler_params=_params(1),
        name="final",
    )(x4, y2, y2, route, pp, ps, gple, wproj, wgate, gfin)


def kernel(x_prompt, x_sample, state_pool, cache_k, cache_v, p_prompt, p_sample, g_mix, g_ffn, g_ple, g_kv, g_final,
           w_pool, pool_scale, w_q, b_q, w_k, b_k, w_v, b_v, w_o, sinks, w1_dense, w3_dense, w2_dense, w_router,
           b_router, w1_moe, w3_moe, w2_moe, w_ple_proj, w_ple_gate):
    bp, sp, d = x_prompt.shape
    bs, ss, _ = x_sample.shape
    depth = g_mix.shape[0]
    assert depth == 2 and w_pool.shape[0] == 1 and w_q.shape[0] == 1 and w1_dense.shape[0] == 1 and w1_moe.shape[0] == 1
    tp, ts = bp * sp, bs * ss
    n_kv, hd = cache_k.shape[2], cache_k.shape[3]
    n_exp = w_router.shape[2]
    ple = p_prompt.shape[-1]
    bf = lambda w: w.astype(BF16)
    vec = lambda g: g.reshape(1, -1)

    x1_p, x1_s, hf_p, hf_s, pool_p, pool_s = _pool_front(
        x_prompt, x_sample, state_pool[0], vec(g_mix[0]), vec(g_ffn[0]), bf(w_pool[0]), vec(pool_scale[0]),
        ta=512, nbs=8)

    x3, k, v, q = _ffn0_post(
        hf_p, hf_s, x1_p, x1_s, p_prompt[0].reshape(tp, ple), p_sample[0].reshape(ts, ple),
        bf(w1_dense[0]), bf(w3_dense[0]), bf(w2_dense[0]),
        vec(g_ple[0]), bf(w_ple_proj[0]), bf(w_ple_gate[0]),
        vec(g_kv), bf(w_k), vec(b_k), bf(w_v), vec(b_v),
        vec(g_mix[1]), bf(w_q[0]), vec(b_q[0]), tm=512, fc=512)

    o_p, o_s, kp, vp, ks, vs = _swa(q, k, v, cache_k, cache_v, sinks[0], bp=bp, sp=sp, bs=bs, ss=ss,
                                    tq=N_BACK * CHUNK)

    wr = jnp.pad(w_router[0], ((0, 0), (0, ROUTE_LANES - n_exp)))
    br = jnp.pad(b_router[0], (0, ROUTE_LANES - n_exp), constant_values=NEG).reshape(1, ROUTE_LANES)
    x4, hm, route = _oproj_route(o_p, o_s, x3, bf(w_o[0]), vec(g_ffn[1]), wr, br, tm=512)

    y2 = _moe(hm, route, bf(w1_moe[0]), bf(w3_moe[0]), bf(w2_moe[0]), tm=256, fc=512)

    y_p, y_s = _final(x4, y2, route, p_prompt[1].reshape(tp, ple), p_sample[1].reshape(ts, ple),
                      vec(g_ple[1]), bf(w_ple_proj[1]), bf(w_ple_gate[1]), vec(g_final), tm=512)

    back = N_BACK * CHUNK
    return (y_p.reshape(bp, sp, d), y_s.reshape(bs, ss, d), pool_p, pool_s,
            kp[:, -back:].reshape(bp, back, n_kv, hd), vp[:, -back:].reshape(bp, back, n_kv, hd),
            ks[:, -back:].reshape(bs, back, n_kv, hd), vs[:, -back:].reshape(bs, back, n_kv, hd))
```

```python
import functools

import jax
import jax.numpy as jnp
from jax import lax
from jax.experimental import pallas as pl
from jax.experimental.pallas import tpu as pltpu

EPS = 1e-6
NEG = -1e30
CHUNK = 64
N_BACK = 2
POOL_WINDOWS = (2, 4, 8, 16)
POOL_BUF = 15
HIST = 16
TOP_K = 2
ROUTE_LANES = 128
VMEM_LIMIT = 56 * 1024 * 1024

F32 = jnp.float32
BF16 = jnp.bfloat16


def _rms(x, g):
    return (x * lax.rsqrt(jnp.mean(x * x, axis=-1, keepdims=True) + EPS)) * g


def _dot(a, b):
    return jnp.dot(a, b, preferred_element_type=F32)


def _const_spec(shape):
    nd = len(shape)
    return pl.BlockSpec(shape, lambda *_: (0,) * nd, pipeline_mode=pl.Buffered(1))


def _params(n_axes):
    return pltpu.CompilerParams(dimension_semantics=("arbitrary",) * n_axes, vmem_limit_bytes=VMEM_LIMIT)


def _pool_front_kernel(*refs, prompt, nb, ta, d):
    (x_ref, hist_ref, gmix_ref, gffn_ref, wpool_ref, scale_ref,
     x1_ref, hf_ref, state_ref, e_ref, f2_ref, f4_ref, f8_ref) = refs
    gw = d // len(POOL_WINDOWS)
    j = pl.program_id(1)
    n = ta + HIST
    x = x_ref[...]
    g = gmix_ref[...]
    h = _rms(x, g)
    if prompt:
        hp = _rms(hist_ref[...], g)
        hp = jnp.where(j == 0, 0.0, hp)
        e_ref[:, 0:HIST, :] = hp
    else:
        e_ref[:, 0:1, :] = jnp.zeros((nb, 1, d), F32)
        e_ref[:, 1:HIST, :] = hist_ref[...]
    e_ref[:, HIST:n, :] = h
    zero_tail = jnp.zeros((nb, 8, d), F32)
    e_ref[:, n:n + 8, :] = zero_tail
    f2_ref[:, 0:n, :] = e_ref[:, 0:n, :] + e_ref[:, 1:n + 1, :]
    f2_ref[:, n:n + 8, :] = zero_tail
    f4_ref[:, 0:n, :] = f2_ref[:, 0:n, gw:] + f2_ref[:, 2:n + 2, gw:]
    f4_ref[:, n:n + 8, :] = zero_tail[:, :, gw:]
    f8_ref[:, 0:n, :] = f4_ref[:, 0:n, gw:] + f4_ref[:, 4:n + 4, gw:]
    f8_ref[:, n:n + 8, :] = zero_tail[:, :, 2 * gw:]
    sums = (
        f2_ref[:, HIST - 1:HIST - 1 + ta, 0:gw],
        f4_ref[:, HIST - 3:HIST - 3 + ta, 0:gw],
        f8_ref[:, HIST - 7:HIST - 7 + ta, 0:gw],
        f8_ref[:, HIST - 15:HIST - 15 + ta, gw:] + f8_ref[:, HIST - 7:HIST - 7 + ta, gw:],
    )
    t_idx = lax.broadcasted_iota(jnp.int32, (1, ta, 1), 1) + j * ta
    ys = []
    for gi, w in enumerate(POOL_WINDOWS):
        if prompt:
            cnt = jnp.minimum(t_idx + 1, w).astype(F32)
            mean = sums[gi] / cnt
        else:
            mean = sums[gi] * (1.0 / w)
        diff = mean - h[:, :, gi * gw:(gi + 1) * gw]
        ys.append(_dot(diff.reshape(nb * ta, gw).astype(BF16), wpool_ref[gi]))
    y = jnp.concatenate(ys, axis=-1) * scale_ref[...]
    x1 = x.reshape(nb * ta, d) + y
    x1_ref[...] = x1
    hf_ref[...] = _rms(x1, gffn_ref[...]).astype(BF16)
    if prompt:
        @pl.when(j == pl.num_programs(1) - 1)
        def _():
            state_ref[0] = e_ref[:, n - POOL_BUF:n, :]
    else:
        state_ref[0] = e_ref[:, n - POOL_BUF:n, :]


def _pool_front(x_p, x_s, state, gmix, gffn, wpool, scale, *, ta, nbs):
    bp, sp, d = x_p.shape
    bs, ss, _ = x_s.shape
    tp, ts = bp * sp, bs * ss
    gw = d // len(POOL_WINDOWS)
    assert sp % ta == 0 and ta % HIST == 0 and bs % nbs == 0
    vec = pl.BlockSpec((1, d), lambda *_: (0, 0))
    wspec = pl.BlockSpec((len(POOL_WINDOWS), gw, gw), lambda *_: (0, 0, 0))

    def scratch(nb, rows):
        return [pltpu.VMEM((nb, rows + HIST + 8, d), F32), pltpu.VMEM((nb, rows + HIST + 8, d), F32),
                pltpu.VMEM((nb, rows + HIST + 8, d - gw), F32), pltpu.VMEM((nb, rows + HIST + 8, d - 2 * gw), F32)]

    njp = sp // ta
    x1_p, hf_p, state_p = pl.pallas_call(
        functools.partial(_pool_front_kernel, prompt=True, nb=1, ta=ta, d=d),
        grid=(bp, njp),
        in_specs=[
            pl.BlockSpec((1, ta, d), lambda b, j: (b, j, 0)),
            pl.BlockSpec((1, HIST, d), lambda b, j: (b, jnp.maximum(j * (ta // HIST) - 1, 0), 0)),
            vec, vec, wspec, vec,
        ],
        out_specs=[
            pl.BlockSpec((ta, d), lambda b, j: (b * njp + j, 0)),
            pl.BlockSpec((ta, d), lambda b, j: (b * njp + j, 0)),
            pl.BlockSpec((1, 1, POOL_BUF, d), lambda b, j: (0, b, 0, 0)),
        ],
        out_shape=[jax.ShapeDtypeStruct((tp, d), F32), jax.ShapeDtypeStruct((tp, d), BF16),
                   jax.ShapeDtypeStruct((1, bp, POOL_BUF, d), F32)],
        scratch_shapes=scratch(1, ta),
        compiler_params=_params(2),
        name="pool_front_prompt",
    )(x_p, x_p, gmix, gffn, wpool, scale)

    rows = nbs * ss
    x1_s, hf_s, state_s = pl.pallas_call(
        functools.partial(_pool_front_kernel, prompt=False, nb=nbs, ta=ss, d=d),
        grid=(bs // nbs, 1),
        in_specs=[
            pl.BlockSpec((nbs, ss, d), lambda b, j: (b, 0, 0)),
            pl.BlockSpec((nbs, POOL_BUF, d), lambda b, j: (b, 0, 0)),
            vec, vec, wspec, vec,
        ],
        out_specs=[
            pl.BlockSpec((rows, d), lambda b, j: (b, 0)),
            pl.BlockSpec((rows, d), lambda b, j: (b, 0)),
            pl.BlockSpec((1, nbs, POOL_BUF, d), lambda b, j: (0, b, 0, 0)),
        ],
        out_shape=[jax.ShapeDtypeStruct((ts, d), F32), jax.ShapeDtypeStruct((ts, d), BF16),
                   jax.ShapeDtypeStruct((1, bs, POOL_BUF, d), F32)],
        scratch_shapes=scratch(nbs, ss),
        compiler_params=_params(2),
        name="pool_front_sample",
    )(x_s, state, gmix, gffn, wpool, scale)
    return x1_p, x1_s, hf_p, hf_s, state_p, state_s


def _swiglu_acc(x_bf, w1_ref, w3_ref, w2_ref, acc_ref, fc, after_chunk=None):
    d_ff = w1_ref.shape[-1]
    n_chunks = d_ff // fc
    for c in range(n_chunks):
        sl = slice(c * fc, (c + 1) * fc)
        h1 = _dot(x_bf, w1_ref[:, sl])
        h3 = _dot(x_bf, w3_ref[:, sl])
        a = (h1 * jax.nn.sigmoid(h1)) * h3
        acc_ref[...] += _dot(a.astype(BF16), w2_ref[sl, :])
        if after_chunk is not None:
            after_chunk(c, n_chunks)


def _ple_gate(x, p_bf, gple, wproj_ref, wgate_ref):
    e = _dot(p_bf, wproj_ref[...])
    gate = jax.nn.sigmoid(_dot(_rms(x, gple).astype(BF16), wgate_ref[...]))
    return x + gate * e


def _ffn0_post_kernel(hfp_ref, hfs_ref, x1p_ref, x1s_ref, pp_ref, ps_ref, w1_ref, w3_ref, w2_ref,
                      gple_ref, wproj_ref, wgate_ref, gkv_ref, wk_ref, bk_ref, wv_ref, bv_ref,
                      gmix_ref, wq_ref, bq_ref,
                      x3_ref, k_ref, v_ref, q_ref, acc_ref, *, fc, n_prompt_tiles):
    is_prompt = pl.program_id(0) < n_prompt_tiles
    acc_ref[...] = jnp.where(is_prompt, x1p_ref[...], x1s_ref[...])
    _swiglu_acc(jnp.where(is_prompt, hfp_ref[...], hfs_ref[...]), w1_ref, w3_ref, w2_ref, acc_ref, fc)
    p = jnp.where(is_prompt, pp_ref[...], ps_ref[...]).astype(BF16)
    x3 = _ple_gate(acc_ref[...], p, gple_ref[...], wproj_ref, wgate_ref)
    x3_ref[...] = x3
    kv_in = _rms(x3, gkv_ref[...]).astype(BF16)
    k_ref[...] = _dot(kv_in, wk_ref[...]) + bk_ref[...]
    v_ref[...] = _dot(kv_in, wv_ref[...]) + bv_ref[...]
    hq = _rms(x3, gmix_ref[...]).astype(BF16)
    q_ref[...] = (_dot(hq, wq_ref[...]) + bq_ref[...]).astype(BF16)


def _split_specs(tm, width, n_prompt_tiles, layer=None):
    if layer is None:
        return [pl.BlockSpec((tm, width), lambda i, *_: (jnp.minimum(i, n_prompt_tiles - 1), 0)),
                pl.BlockSpec((tm, width), lambda i, *_: (jnp.maximum(i - n_prompt_tiles, 0), 0))]
    return [pl.BlockSpec((None, tm, width), lambda i, *_: (layer, jnp.minimum(i, n_prompt_tiles - 1), 0)),
            pl.BlockSpec((None, tm, width), lambda i, *_: (layer, jnp.maximum(i - n_prompt_tiles, 0), 0))]


def _ffn0_post(hf_p, hf_s, x1_p, x1_s, pp, ps, w1, w3, w2, gple, wproj, wgate, gkv, wk, bk, wv, bv, gmix, wq, bq,
               *, tm, fc):
    tp, d = x1_p.shape
    ts = x1_s.shape[0]
    t = tp + ts
    kvw = wk.shape[1]
    assert ts % tm == 0 and tp % tm == 0 and w1.shape[1] % fc == 0
    npt = tp // tm
    row = lambda w: pl.BlockSpec((tm, w), lambda i: (i, 0))
    vec = lambda w: pl.BlockSpec((1, w), lambda i: (0, 0))
    return pl.pallas_call(
        functools.partial(_ffn0_post_kernel, fc=fc, n_prompt_tiles=npt),
        grid=(t // tm,),
        in_specs=_split_specs(tm, d, npt) + _split_specs(tm, d, npt) + _split_specs(tm, pp.shape[2], npt, layer=0) + [
            _const_spec(w1.shape), _const_spec(w3.shape), _const_spec(w2.shape),
            vec(d), _const_spec(wproj.shape), _const_spec(wgate.shape),
            vec(d), _const_spec(wk.shape), vec(kvw), _const_spec(wv.shape), vec(kvw),
            vec(d), _const_spec(wq.shape), vec(d),
        ],
        out_specs=[row(d), row(kvw), row(kvw), row(d)],
        out_shape=[jax.ShapeDtypeStruct((t, d), F32), jax.ShapeDtypeStruct((t, kvw), F32),
                   jax.ShapeDtypeStruct((t, kvw), F32), jax.ShapeDtypeStruct((t, d), BF16)],
        scratch_shapes=[pltpu.VMEM((tm, d), F32)],
        compiler_params=_params(1),
        name="ffn0_post",
    )(hf_p, hf_s, x1_p, x1_s, pp, ps, w1, w3, w2, gple, wproj, wgate, gkv, wk, bk, wv, bv, gmix, wq, bq)


def _swa_kernel(*refs, prompt, tq, n_kv, group, hd):
    if prompt:
        sink_ref, q_ref, ka_ref, kb_ref, va_ref, vb_ref, o_ref = refs
        kk = jnp.concatenate([ka_ref[...], kb_ref[...]], axis=0)
        vv = jnp.concatenate([va_ref[...], vb_ref[...]], axis=0)
        first_chunk = N_BACK - pl.program_id(1) * (tq // CHUNK)
    else:
        sink_ref, q_ref, ck_ref, kn_ref, cv_ref, vn_ref, o_ref = refs
        pad = jnp.zeros(kn_ref.shape, F32)
        kk = jnp.concatenate([ck_ref[0], kn_ref[...], pad], axis=0)
        vv = jnp.concatenate([cv_ref[0], vn_ref[...], pad], axis=0)
        first_chunk = 0
    nk = kk.shape[0]
    lanes = 128
    per = lanes // hd
    assert per == 2 and group % per == 0 and nk % lanes == 0
    qc = lax.broadcasted_iota(jnp.int32, (tq, per * nk), 0) // CHUNK
    kc = (lax.broadcasted_iota(jnp.int32, (tq, per * nk), 1) % nk) // CHUNK
    bias = jnp.where(kc >= jnp.maximum(qc, first_chunk), jnp.where(kc <= qc + N_BACK, 0.0, NEG), NEG)
    lane = lax.broadcasted_iota(jnp.int32, (nk, lanes), 1)
    ones_a = jnp.where(lane < hd, 1.0, 0.0)
    ones_b = jnp.where(lane >= hd, 1.0, 0.0)
    out_lane = lax.broadcasted_iota(jnp.int32, (tq, lanes), 1)
    for kvh in range(n_kv):
        tile, half = divmod(kvh, per)
        keep = (lane < hd) if half == 0 else (lane >= hd)

        def both_halves(x):
            own = jnp.where(keep, x[:, tile * lanes:(tile + 1) * lanes], 0.0)
            other = pltpu.roll(own, hd, axis=1)
            return (own, other) if half == 0 else (other, own)

        k_a, k_b = both_halves(kk * (hd ** -0.5))
        v_a, v_b = both_halves(vv)
        kbd = jnp.concatenate([k_a, k_b], axis=0).astype(BF16)
        vobd = jnp.concatenate([jnp.concatenate([v_a, ones_a], axis=1),
                                jnp.concatenate([v_b, ones_b], axis=1)], axis=0).astype(BF16)
        for pr in range(group // per):
            head_a = kvh * group + pr * per
            col0 = head_a * hd
            s = lax.dot_general(q_ref[:, col0:col0 + lanes], kbd, (((1,), (1,)), ((), ())),
                                preferred_element_type=F32) + bias
            sink_a, sink_b = sink_ref[head_a], sink_ref[head_a + 1]
            m_a = jnp.maximum(jnp.max(s[:, :nk], axis=-1, keepdims=True), sink_a)
            m_b = jnp.maximum(jnp.max(s[:, nk:], axis=-1, keepdims=True), sink_b)
            p = jnp.concatenate([jnp.exp(s[:, :nk] - m_a), jnp.exp(s[:, nk:] - m_b)], axis=1).astype(BF16)
            r = _dot(p, vobd)
            sink_term = jnp.where(out_lane < hd, jnp.exp(sink_a - m_a), jnp.exp(sink_b - m_b))
            o_ref[:, col0:col0 + lanes] = (r[:, :lanes] / (r[:, lanes:] + sink_term)).astype(BF16)


def _swa(q, k, v, cache_k, cache_v, sinks, *, bp, sp, bs, ss, tq):
    t, d = q.shape
    kvw = k.shape[1]
    tp = bp * sp
    n_heads = sinks.shape[0]
    hd = d // n_heads
    n_kv = kvw // hd
    back = N_BACK * CHUNK
    assert tq == back and sp % tq == 0 and ss == CHUNK and cache_k.shape[1] == back
    smem = pl.BlockSpec(memory_space=pltpu.SMEM)
    nj = sp // tq
    kw = dict(n_kv=n_kv, group=n_heads // n_kv, hd=hd)
    prev_blk = pl.BlockSpec((tq, kvw), lambda b, j: (b * nj + jnp.maximum(j - 1, 0), 0))
    this_blk = pl.BlockSpec((tq, kvw), lambda b, j: (b * nj + j, 0))
    o_p = pl.pallas_call(
        functools.partial(_swa_kernel, prompt=True, tq=tq, **kw),
        grid=(bp, nj),
        in_specs=[smem, pl.BlockSpec((tq, d), lambda b, j: (b * nj + j, 0)), prev_blk, this_blk, prev_blk, this_blk],
        out_specs=pl.BlockSpec((tq, d), lambda b, j: (b * nj + j, 0)),
        out_shape=jax.ShapeDtypeStruct((tp, d), BF16),
        compiler_params=_params(2),
        name="swa_prompt",
    )(sinks, q, k, k, v, v)
    off = tp // ss
    cache_blk = pl.BlockSpec((1, back, kvw), lambda b: (b, 0, 0))
    new_blk = pl.BlockSpec((ss, kvw), lambda b: (off + b, 0))
    o_s = pl.pallas_call(
        functools.partial(_swa_kernel, prompt=False, tq=ss, **kw),
        grid=(bs,),
        in_specs=[smem, pl.BlockSpec((ss, d), lambda b: (off + b, 0)), cache_blk, new_blk, cache_blk, new_blk],
        out_specs=pl.BlockSpec((ss, d), lambda b: (b, 0)),
        out_shape=jax.ShapeDtypeStruct((bs * ss, d), BF16),
        compiler_params=_params(1),
        name="swa_sample",
    )(sinks, q, cache_k.reshape(bs, back, kvw), k, cache_v.reshape(bs, back, kvw), v)
    return o_p, o_s


def _oproj_route_kernel(op_ref, os_ref, x3_ref, wo_ref, gffn_ref, wr_ref, br_ref, x4_ref, hm_ref, route_ref,
                        *, n_prompt_tiles):
    o = jnp.where(pl.program_id(0) < n_prompt_tiles, op_ref[...], os_ref[...])
    x4 = x3_ref[...] + _dot(o, wo_ref[...])
    x4_ref[...] = x4
    hm = _rms(x4, gffn_ref[...])
    hm_ref[...] = hm
    hm_hi = hm.astype(BF16)
    hm_lo = (hm - hm_hi.astype(F32)).astype(BF16)
    wr = wr_ref[...]
    wr_hi = wr.astype(BF16)
    wr_lo = (wr - wr_hi.astype(F32)).astype(BF16)
    logits = (_dot(hm_hi, wr_hi) + (_dot(hm_lo, wr_hi) + _dot(hm_hi, wr_lo))) + br_ref[...]
    lane = lax.broadcasted_iota(jnp.int32, logits.shape, 1).astype(F32)
    m1 = jnp.max(logits, axis=-1, keepdims=True)
    i1 = jnp.min(jnp.where(logits == m1, lane, float(ROUTE_LANES)), axis=-1, keepdims=True)
    rest = jnp.where(lane == i1, -jnp.inf, logits)
    m2 = jnp.max(rest, axis=-1, keepdims=True)
    i2 = jnp.min(jnp.where(rest == m2, lane, float(ROUTE_LANES)), axis=-1, keepdims=True)
    e2 = jnp.exp(m2 - m1)
    g1 = 1.0 / (1.0 + e2)
    g2 = e2 / (1.0 + e2)
    route = jnp.where(lane == 0.0, i1, jnp.where(lane == 1.0, i2,
                      jnp.where(lane == 2.0, g1, jnp.where(lane == 3.0, g2, 0.0))))
    route_ref[...] = route


def _oproj_route(o_p, o_s, x3, wo, gffn, wr, br, *, tm):
    t, d = x3.shape
    tp = o_p.shape[0]
    assert tp % tm == 0 and o_s.shape[0] % tm == 0
    row = lambda w: pl.BlockSpec((tm, w), lambda i: (i, 0))
    return pl.pallas_call(
        functools.partial(_oproj_route_kernel, n_prompt_tiles=tp // tm),
        grid=(t // tm,),
        in_specs=_split_specs(tm, d, tp // tm) + [row(d), _const_spec(wo.shape), pl.BlockSpec((1, d), lambda i: (0, 0)),
                  _const_spec(wr.shape), pl.BlockSpec((1, ROUTE_LANES), lambda i: (0, 0))],
        out_specs=[row(d), row(d), row(ROUTE_LANES)],
        out_shape=[jax.ShapeDtypeStruct((t, d), F32), jax.ShapeDtypeStruct((t, d), F32),
                   jax.ShapeDtypeStruct((t, ROUTE_LANES), F32)],
        compiler_params=_params(1),
        name="oproj_route",
    )(o_p, o_s, x3, wo, gffn, wr, br)


def _moe_kernel(texp_ref, rtok_ref, rdst_ref, hm_ref, w1_ref, w3_ref, w2_ref, y_ref,
                xbuf0, xbuf1, obuf0, obuf1, acc_ref, gsem, ssem, *, tm, fc, n_steps):
    i = pl.program_id(0)

    def gather_row(base, r, xb, sem):
        tok = rtok_ref[base + r]
        pltpu.make_async_copy(hm_ref.at[pl.ds(tok, 1), :], xb.at[pl.ds(r, 1), :], sem).start()

    def scatter_row(base, r, ob, sem):
        dst = rdst_ref[base + r]
        pltpu.make_async_copy(ob.at[pl.ds(r, 1), :], y_ref.at[pl.ds(dst, 1), :], sem).start()

    def wait_gather(xb, sem):
        pltpu.make_async_copy(hm_ref.at[pl.ds(0, tm), :], xb, sem).wait()

    def wait_scatter(ob, sem):
        pltpu.make_async_copy(ob, y_ref.at[pl.ds(0, tm), :], sem).wait()

    @pl.when(i == 0)
    def _():
        obuf0[...] = jnp.zeros(obuf0.shape, F32)
        obuf1[...] = jnp.zeros(obuf1.shape, F32)
        lax.fori_loop(0, tm, lambda r, c: (gather_row(0, r, xbuf0, gsem.at[0]), c)[1], 0)

    def step(xb, ob, xb_next, ob_prev, p):
        wait_gather(xb, gsem.at[p])
        acc_ref[...] = jnp.zeros(acc_ref.shape, F32)
        g_base = (i + 1) * tm
        s_base = i * tm

        def move_rows(c, n_chunks):
            per = -(-2 * tm // n_chunks)
            for m in range(c * per, min((c + 1) * per, 2 * tm)):
                if m < tm:
                    gather_row(g_base, m, xb_next, gsem.at[1 - p])
                else:
                    scatter_row(s_base, m - tm, ob_prev, ssem.at[1 - p])

        _swiglu_acc(xb[...].astype(BF16), w1_ref, w3_ref, w2_ref, acc_ref, fc, after_chunk=move_rows)

        @pl.when(i >= 1)
        def _():
            wait_scatter(ob, ssem.at[p])
        ob[...] = acc_ref[...]

    @pl.when(i % 2 == 0)
    def _():
        step(xbuf0, obuf0, xbuf1, obuf1, 0)

    @pl.when(i % 2 == 1)
    def _():
        step(xbuf1, obuf1, xbuf0, obuf0, 1)

    @pl.when(i == n_steps - 1)
    def _():
        p = (n_steps - 1) % 2
        xb_next, ob_prev = (xbuf1, obuf1) if p == 0 else (xbuf0, obuf0)
        wait_gather(xb_next, gsem.at[1 - p])
        wait_scatter(ob_prev, ssem.at[1 - p])


def _moe(hm, route, w1, w3, w2, *, tm, fc):
    t, d = hm.shape
    n_exp = w1.shape[0]
    n_assign = t * TOP_K
    nt = n_assign // tm + n_exp
    assert n_assign % tm == 0
    e_flat = route[:, :TOP_K].astype(jnp.int32).reshape(-1)
    order = jnp.argsort(e_flat, stable=True).astype(jnp.int32)
    counts = jnp.sum((e_flat[:, None] == jnp.arange(n_exp, dtype=jnp.int32)[None, :]).astype(jnp.int32), axis=0)
    tiles_e = (counts + tm - 1) // tm
    tile_end = jnp.cumsum(tiles_e)
    tile_start = tile_end - tiles_e
    cnt_start = jnp.cumsum(counts) - counts
    tile_ids = jnp.arange(nt, dtype=jnp.int32)
    texp = jnp.minimum(jnp.sum((tile_ids[:, None] >= tile_end[None, :]).astype(jnp.int32), axis=1), n_exp - 1)
    within = (tile_ids - tile_start[texp]) * tm
    nval = jnp.where(tile_ids < tile_end[-1], jnp.clip(counts[texp] - within, 0, tm), 0).astype(jnp.int32)
    r_in = jnp.arange(tm, dtype=jnp.int32)[None, :]
    src = jnp.clip(cnt_start[texp][:, None] + within[:, None] + r_in, 0, n_assign - 1)
    real = r_in < nval[:, None]
    assign = jnp.where(real, order[src], 0)
    scratch_row = TOP_K * t + (tile_ids[:, None] % 2) * tm + r_in
    rtok = jnp.concatenate([(assign // TOP_K).reshape(-1), jnp.zeros((2 * tm,), jnp.int32)])
    rdst = jnp.concatenate([TOP_K * t + tm + r_in[0],
                            jnp.where(real, (assign % TOP_K) * t + assign // TOP_K, scratch_row).reshape(-1)])
    texp = jnp.concatenate([texp, texp[-1:]])
    n_steps = nt + 1

    wspec = lambda shape: pl.BlockSpec((None,) + shape[1:], lambda i, te, *_: (te[i], 0, 0),
                                       pipeline_mode=pl.Buffered(1))
    y2 = pl.pallas_call(
        functools.partial(_moe_kernel, tm=tm, fc=fc, n_steps=n_steps),
        grid_spec=pltpu.PrefetchScalarGridSpec(
            num_scalar_prefetch=3,
            grid=(n_steps,),
            in_specs=[pl.BlockSpec(memory_space=pl.ANY), wspec(w1.shape), wspec(w3.shape), wspec(w2.shape)],
            out_specs=pl.BlockSpec(memory_space=pl.ANY),
            scratch_shapes=[pltpu.VMEM((tm, d), F32)] * 5
            + [pltpu.SemaphoreType.DMA((2,)), pltpu.SemaphoreType.DMA((2,))],
        ),
        out_shape=jax.ShapeDtypeStruct((TOP_K * t + 2 * tm, d), F32),
        compiler_params=_params(1),
        name="moe",
    )(texp, rtok, rdst, hm, w1, w3, w2)
    return y2


def _final_kernel(x4_ref, ya_ref, yb_ref, route_ref, pp_ref, ps_ref, gple_ref, wproj_ref, wgate_ref, gfin_ref,
                  yp_ref, ys_ref, *, n_prompt_tiles):
    i = pl.program_id(0)
    route = route_ref[...]
    g1 = route[:, 2:3]
    g2 = route[:, 3:4]
    x5 = x4_ref[...] + (g1 * ya_ref[...] + g2 * yb_ref[...])
    p = jnp.where(i < n_prompt_tiles, pp_ref[...], ps_ref[...]).astype(BF16)
    x6 = _ple_gate(x5, p, gple_ref[...], wproj_ref, wgate_ref)
    y = _rms(x6, gfin_ref[...])

    @pl.when(i < n_prompt_tiles)
    def _():
        yp_ref[...] = y

    @pl.when(i >= n_prompt_tiles)
    def _():
        ys_ref[...] = y


def _final(x4, y2, route, pp, ps, gple, wproj, wgate, gfin, *, tm):
    t, d = x4.shape
    tp, ts = pp.shape[1], ps.shape[1]
    npt = tp // tm
    assert tp % tm == 0 and ts % tm == 0
    row = lambda w: pl.BlockSpec((tm, w), lambda i: (i, 0))
    vec = pl.BlockSpec((1, d), lambda i: (0, 0))
    return pl.pallas_call(
        functools.partial(_final_kernel, n_prompt_tiles=npt),
        grid=(t // tm,),
        in_specs=[row(d), row(d), pl.BlockSpec((tm, d), lambda i: (t // tm + i, 0)), row(ROUTE_LANES)]
        + _split_specs(tm, pp.shape[2], npt, layer=1) + [vec, _const_spec(wproj.shape), _const_spec(wgate.shape), vec],
        out_specs=_split_specs(tm, d, npt),
        out_shape=[jax.ShapeDtypeStruct((tp, d), F32), jax.ShapeDtypeStruct((ts, d), F32)],
        compiler_params=_params(1),
        name="final",
    )(x4, y2, y2, route, pp, ps, gple, wproj, wgate, gfin)


def kernel(x_prompt, x_sample, state_pool, cache_k, cache_v, p_prompt, p_sample, g_mix, g_ffn, g_ple, g_kv, g_final,
           w_pool, pool_scale, w_q, b_q, w_k, b_k, w_v, b_v, w_o, sinks, w1_dense, w3_dense, w2_dense, w_router,
           b_router, w1_moe, w3_moe, w2_moe, w_ple_proj, w_ple_gate):
    bp, sp, d = x_prompt.shape
    bs, ss, _ = x_sample.shape
    depth = g_mix.shape[0]
    assert depth == 2 and w_pool.shape[0] == 1 and w_q.shape[0] == 1 and w1_dense.shape[0] == 1 and w1_moe.shape[0] == 1
    tp, ts = bp * sp, bs * ss
    n_kv, hd = cache_k.shape[2], cache_k.shape[3]
    n_exp = w_router.shape[2]
    ple = p_prompt.shape[-1]
    bf = lambda w: w.astype(BF16)
    vec = lambda g: g.reshape(1, -1)
    pp = p_prompt.reshape(depth, tp, ple)
    ps = p_sample.reshape(depth, ts, ple)

    x1_p, x1_s, hf_p, hf_s, pool_p, pool_s = _pool_front(
        x_prompt, x_sample, state_pool[0], vec(g_mix[0]), vec(g_ffn[0]), bf(w_pool[0]), vec(pool_scale[0]),
        ta=512, nbs=8)

    x3, k, v, q = _ffn0_post(
        hf_p, hf_s, x1_p, x1_s, pp, ps,
        bf(w1_dense[0]), bf(w3_dense[0]), bf(w2_dense[0]),
        vec(g_ple[0]), bf(w_ple_proj[0]), bf(w_ple_gate[0]),
        vec(g_kv), bf(w_k), vec(b_k), bf(w_v), vec(b_v),
        vec(g_mix[1]), bf(w_q[0]), vec(b_q[0]), tm=512, fc=512)

    o_p, o_s = _swa(q, k, v, cache_k, cache_v, sinks[0], bp=bp, sp=sp, bs=bs, ss=ss, tq=N_BACK * CHUNK)

    wr = jnp.pad(w_router[0], ((0, 0), (0, ROUTE_LANES - n_exp)))
    br = jnp.pad(b_router[0], (0, ROUTE_LANES - n_exp), constant_values=NEG).reshape(1, ROUTE_LANES)
    x4, hm, route = _oproj_route(o_p, o_s, x3, bf(w_o[0]), vec(g_ffn[1]), wr, br, tm=512)

    y2 = _moe(hm, route, bf(w1_moe[0]), bf(w3_moe[0]), bf(w2_moe[0]), tm=256, fc=512)

    y_p, y_s = _final(x4, y2, route, pp, ps,
                      vec(g_ple[1]), bf(w_ple_proj[1]), bf(w_ple_gate[1]), vec(g_final), tm=512)

    back = N_BACK * CHUNK

    def new_cache(rows, cache):
        b = rows.shape[0] // (sp if cache is None else ss)
        rows = rows.reshape(b, -1, n_kv, hd)
        if cache is not None:
            rows = jnp.concatenate([cache, rows], axis=1)
        return rows[:, -back:]

    return (y_p.reshape(bp, sp, d), y_s.reshape(bs, ss, d), pool_p, pool_s,
            new_cache(k[:tp], None), new_cache(v[:tp], None), new_cache(k[tp:], cache_k), new_cache(v[tp:], cache_v))
```

```python
import functools

import jax
import jax.numpy as jnp
from jax import lax
from jax.experimental import pallas as pl
from jax.experimental.pallas import tpu as pltpu

EPS = 1e-6
NEG = -1e30
CHUNK = 64
N_BACK = 2
POOL_WINDOWS = (2, 4, 8, 16)
POOL_BUF = 15
HIST = 16
TOP_K = 2
ROUTE_LANES = 128
VMEM_LIMIT = 56 * 1024 * 1024

F32 = jnp.float32
BF16 = jnp.bfloat16


def _rms(x, g):
    return (x * lax.rsqrt(jnp.mean(x * x, axis=-1, keepdims=True) + EPS)) * g


def _dot(a, b):
    return jnp.dot(a, b, preferred_element_type=F32)


def _const_spec(shape):
    nd = len(shape)
    return pl.BlockSpec(shape, lambda *_: (0,) * nd, pipeline_mode=pl.Buffered(1))


def _params(n_axes):
    return pltpu.CompilerParams(dimension_semantics=("arbitrary",) * n_axes, vmem_limit_bytes=VMEM_LIMIT)


def _pool_front_kernel(*refs, prompt, nb, ta, d):
    (x_ref, hist_ref, gmix_ref, gffn_ref, wpool_ref, scale_ref,
     x1_ref, hf_ref, state_ref, e_ref, f2_ref, f4_ref, f8_ref) = refs
    gw = d // len(POOL_WINDOWS)
    j = pl.program_id(1)
    n = ta + HIST
    x = x_ref[...]
    g = gmix_ref[...]
    h = _rms(x, g)
    if prompt:
        hp = _rms(hist_ref[...], g)
        hp = jnp.where(j == 0, 0.0, hp)
        e_ref[:, 0:HIST, :] = hp
    else:
        e_ref[:, 0:1, :] = jnp.zeros((nb, 1, d), F32)
        e_ref[:, 1:HIST, :] = hist_ref[...]
    e_ref[:, HIST:n, :] = h
    zero_tail = jnp.zeros((nb, 8, d), F32)
    e_ref[:, n:n + 8, :] = zero_tail
    f2_ref[:, 0:n, :] = e_ref[:, 0:n, :] + e_ref[:, 1:n + 1, :]
    f2_ref[:, n:n + 8, :] = zero_tail
    f4_ref[:, 0:n, :] = f2_ref[:, 0:n, gw:] + f2_ref[:, 2:n + 2, gw:]
    f4_ref[:, n:n + 8, :] = zero_tail[:, :, gw:]
    f8_ref[:, 0:n, :] = f4_ref[:, 0:n, gw:] + f4_ref[:, 4:n + 4, gw:]
    f8_ref[:, n:n + 8, :] = zero_tail[:, :, 2 * gw:]
    sums = (
        f2_ref[:, HIST - 1:HIST - 1 + ta, 0:gw],
        f4_ref[:, HIST - 3:HIST - 3 + ta, 0:gw],
        f8_ref[:, HIST - 7:HIST - 7 + ta, 0:gw],
        f8_ref[:, HIST - 15:HIST - 15 + ta, gw:] + f8_ref[:, HIST - 7:HIST - 7 + ta, gw:],
    )
    t_idx = lax.broadcasted_iota(jnp.int32, (1, ta, 1), 1) + j * ta
    ys = []
    for gi, w in enumerate(POOL_WINDOWS):
        if prompt:
            cnt = jnp.minimum(t_idx + 1, w).astype(F32)
            mean = sums[gi] / cnt
        else:
            mean = sums[gi] * (1.0 / w)
        diff = mean - h[:, :, gi * gw:(gi + 1) * gw]
        ys.append(_dot(diff.reshape(nb * ta, gw).astype(BF16), wpool_ref[gi]))
    y = jnp.concatenate(ys, axis=-1) * scale_ref[...]
    x1 = x.reshape(nb * ta, d) + y
    x1_ref[...] = x1
    hf_ref[...] = _rms(x1, gffn_ref[...]).astype(BF16)
    if prompt:
        @pl.when(j == pl.num_programs(1) - 1)
        def _():
            state_ref[0] = e_ref[:, n - POOL_BUF:n, :]
    else:
        state_ref[0] = e_ref[:, n - POOL_BUF:n, :]


def _pool_front(x_p, x_s, state, gmix, gffn, wpool, scale, *, ta, nbs):
    bp, sp, d = x_p.shape
    bs, ss, _ = x_s.shape
    tp, ts = bp * sp, bs * ss
    gw = d // len(POOL_WINDOWS)
    assert sp % ta == 0 and ta % HIST == 0 and bs % nbs == 0
    vec = pl.BlockSpec((1, d), lambda *_: (0, 0))
    wspec = pl.BlockSpec((len(POOL_WINDOWS), gw, gw), lambda *_: (0, 0, 0))

    def scratch(nb, rows):
        return [pltpu.VMEM((nb, rows + HIST + 8, d), F32), pltpu.VMEM((nb, rows + HIST + 8, d), F32),
                pltpu.VMEM((nb, rows + HIST + 8, d - gw), F32), pltpu.VMEM((nb, rows + HIST + 8, d - 2 * gw), F32)]

    njp = sp // ta
    x1_p, hf_p, state_p = pl.pallas_call(
        functools.partial(_pool_front_kernel, prompt=True, nb=1, ta=ta, d=d),
        grid=(bp, njp),
        in_specs=[
            pl.BlockSpec((1, ta, d), lambda b, j: (b, j, 0)),
            pl.BlockSpec((1, HIST, d), lambda b, j: (b, jnp.maximum(j * (ta // HIST) - 1, 0), 0)),
            vec, vec, wspec, vec,
        ],
        out_specs=[
            pl.BlockSpec((ta, d), lambda b, j: (b * njp + j, 0)),
            pl.BlockSpec((ta, d), lambda b, j: (b * njp + j, 0)),
            pl.BlockSpec((1, 1, POOL_BUF, d), lambda b, j: (0, b, 0, 0)),
        ],
        out_shape=[jax.ShapeDtypeStruct((tp, d), F32), jax.ShapeDtypeStruct((tp, d), BF16),
                   jax.ShapeDtypeStruct((1, bp, POOL_BUF, d), F32)],
        scratch_shapes=scratch(1, ta),
        compiler_params=_params(2),
        name="pool_front_prompt",
    )(x_p, x_p, gmix, gffn, wpool, scale)

    rows = nbs * ss
    x1_s, hf_s, state_s = pl.pallas_call(
        functools.partial(_pool_front_kernel, prompt=False, nb=nbs, ta=ss, d=d),
        grid=(bs // nbs, 1),
        in_specs=[
            pl.BlockSpec((nbs, ss, d), lambda b, j: (b, 0, 0)),
            pl.BlockSpec((nbs, POOL_BUF, d), lambda b, j: (b, 0, 0)),
            vec, vec, wspec, vec,
        ],
        out_specs=[
            pl.BlockSpec((rows, d), lambda b, j: (b, 0)),
            pl.BlockSpec((rows, d), lambda b, j: (b, 0)),
            pl.BlockSpec((1, nbs, POOL_BUF, d), lambda b, j: (0, b, 0, 0)),
        ],
        out_shape=[jax.ShapeDtypeStruct((ts, d), F32), jax.ShapeDtypeStruct((ts, d), BF16),
                   jax.ShapeDtypeStruct((1, bs, POOL_BUF, d), F32)],
        scratch_shapes=scratch(nbs, ss),
        compiler_params=_params(2),
        name="pool_front_sample",
    )(x_s, state, gmix, gffn, wpool, scale)
    return x1_p, x1_s, hf_p, hf_s, state_p, state_s


def _swiglu_acc(x_bf, w1_ref, w3_ref, w2_ref, acc_ref, fc, after_chunk=None):
    d_ff = w1_ref.shape[-1]
    n_chunks = d_ff // fc
    for c in range(n_chunks):
        sl = slice(c * fc, (c + 1) * fc)
        h1 = _dot(x_bf, w1_ref[:, sl])
        h3 = _dot(x_bf, w3_ref[:, sl])
        a = (h1 * jax.nn.sigmoid(h1)) * h3
        acc_ref[...] += _dot(a.astype(BF16), w2_ref[sl, :])
        if after_chunk is not None:
            after_chunk(c, n_chunks)


def _ple_gate(x, p_bf, gple, wproj_ref, wgate_ref):
    e = _dot(p_bf, wproj_ref[...])
    gate = jax.nn.sigmoid(_dot(_rms(x, gple).astype(BF16), wgate_ref[...]))
    return x + gate * e


def _ffn0_post_kernel(hfp_ref, hfs_ref, x1p_ref, x1s_ref, pp_ref, ps_ref, w1_ref, w3_ref, w2_ref,
                      gple_ref, wproj_ref, wgate_ref, gkv_ref, wk_ref, bk_ref, wv_ref, bv_ref,
                      gmix_ref, wq_ref, bq_ref,
                      x3_ref, k_ref, v_ref, q_ref, acc_ref, *, fc, n_prompt_tiles):
    is_prompt = pl.program_id(0) < n_prompt_tiles
    acc_ref[...] = jnp.where(is_prompt, x1p_ref[...], x1s_ref[...])
    _swiglu_acc(jnp.where(is_prompt, hfp_ref[...], hfs_ref[...]), w1_ref, w3_ref, w2_ref, acc_ref, fc)
    p = jnp.where(is_prompt, pp_ref[...], ps_ref[...]).astype(BF16)
    x3 = _ple_gate(acc_ref[...], p, gple_ref[...], wproj_ref, wgate_ref)
    x3_ref[...] = x3
    kv_in = _rms(x3, gkv_ref[...]).astype(BF16)
    k_ref[...] = _dot(kv_in, wk_ref[...]) + bk_ref[...]
    v_ref[...] = _dot(kv_in, wv_ref[...]) + bv_ref[...]
    hq = _rms(x3, gmix_ref[...]).astype(BF16)
    q_ref[...] = (_dot(hq, wq_ref[...]) + bq_ref[...]).astype(BF16)


def _split_specs(tm, width, n_prompt_tiles, layer=None):
    if layer is None:
        return [pl.BlockSpec((tm, width), lambda i, *_: (jnp.minimum(i, n_prompt_tiles - 1), 0)),
                pl.BlockSpec((tm, width), lambda i, *_: (jnp.maximum(i - n_prompt_tiles, 0), 0))]
    return [pl.BlockSpec((None, tm, width), lambda i, *_: (layer, jnp.minimum(i, n_prompt_tiles - 1), 0)),
            pl.BlockSpec((None, tm, width), lambda i, *_: (layer, jnp.maximum(i - n_prompt_tiles, 0), 0))]


def _ffn0_post(hf_p, hf_s, x1_p, x1_s, pp, ps, w1, w3, w2, gple, wproj, wgate, gkv, wk, bk, wv, bv, gmix, wq, bq,
               *, tm, fc):
    tp, d = x1_p.shape
    ts = x1_s.shape[0]
    t = tp + ts
    kvw = wk.shape[1]
    assert ts % tm == 0 and tp % tm == 0 and w1.shape[1] % fc == 0
    npt = tp // tm
    row = lambda w: pl.BlockSpec((tm, w), lambda i: (i, 0))
    vec = lambda w: pl.BlockSpec((1, w), lambda i: (0, 0))
    return pl.pallas_call(
        functools.partial(_ffn0_post_kernel, fc=fc, n_prompt_tiles=npt),
        grid=(t // tm,),
        in_specs=_split_specs(tm, d, npt) + _split_specs(tm, d, npt) + _split_specs(tm, pp.shape[2], npt, layer=0) + [
            _const_spec(w1.shape), _const_spec(w3.shape), _const_spec(w2.shape),
            vec(d), _const_spec(wproj.shape), _const_spec(wgate.shape),
            vec(d), _const_spec(wk.shape), vec(kvw), _const_spec(wv.shape), vec(kvw),
            vec(d), _const_spec(wq.shape), vec(d),
        ],
        out_specs=[row(d), row(kvw), row(kvw), row(d)],
        out_shape=[jax.ShapeDtypeStruct((t, d), F32), jax.ShapeDtypeStruct((t, kvw), F32),
                   jax.ShapeDtypeStruct((t, kvw), F32), jax.ShapeDtypeStruct((t, d), BF16)],
        scratch_shapes=[pltpu.VMEM((tm, d), F32)],
        compiler_params=_params(1),
        name="ffn0_post",
    )(hf_p, hf_s, x1_p, x1_s, pp, ps, w1, w3, w2, gple, wproj, wgate, gkv, wk, bk, wv, bv, gmix, wq, bq)


def _swa_kernel(*refs, prompt, tq, n_kv, group, hd):
    if prompt:
        sink_ref, q_ref, ka_ref, kb_ref, va_ref, vb_ref, o_ref = refs
        kk = jnp.concatenate([ka_ref[...], kb_ref[...]], axis=0)
        vv = jnp.concatenate([va_ref[...], vb_ref[...]], axis=0)
        first_chunk = N_BACK - pl.program_id(1) * (tq // CHUNK)
    else:
        sink_ref, q_ref, ck_ref, kn_ref, cv_ref, vn_ref, o_ref = refs
        pad = jnp.zeros(kn_ref.shape, F32)
        kk = jnp.concatenate([ck_ref[0], kn_ref[...], pad], axis=0)
        vv = jnp.concatenate([cv_ref[0], vn_ref[...], pad], axis=0)
        first_chunk = 0
    nk = kk.shape[0]
    lanes = 128
    per = lanes // hd
    assert per == 2 and group % per == 0 and nk % lanes == 0
    qc = lax.broadcasted_iota(jnp.int32, (tq, per * nk), 0) // CHUNK
    kc = (lax.broadcasted_iota(jnp.int32, (tq, per * nk), 1) % nk) // CHUNK
    bias = jnp.where(kc >= jnp.maximum(qc, first_chunk), jnp.where(kc <= qc + N_BACK, 0.0, NEG), NEG)
    lane = lax.broadcasted_iota(jnp.int32, (nk, lanes), 1)
    ones_a = jnp.where(lane < hd, 1.0, 0.0)
    ones_b = jnp.where(lane >= hd, 1.0, 0.0)
    out_lane = lax.broadcasted_iota(jnp.int32, (tq, lanes), 1)
    for kvh in range(n_kv):
        tile, half = divmod(kvh, per)
        keep = (lane < hd) if half == 0 else (lane >= hd)

        def both_halves(x):
            own = jnp.where(keep, x[:, tile * lanes:(tile + 1) * lanes], 0.0)
            other = pltpu.roll(own, hd, axis=1)
            return (own, other) if half == 0 else (other, own)

        k_a, k_b = both_halves(kk * (hd ** -0.5))
        v_a, v_b = both_halves(vv)
        kbd = jnp.concatenate([k_a, k_b], axis=0).astype(BF16)
        vobd = jnp.concatenate([jnp.concatenate([v_a, ones_a], axis=1),
                                jnp.concatenate([v_b, ones_b], axis=1)], axis=0).astype(BF16)
        for pr in range(group // per):
            head_a = kvh * group + pr * per
            col0 = head_a * hd
            s = lax.dot_general(q_ref[:, col0:col0 + lanes], kbd, (((1,), (1,)), ((), ())),
                                preferred_element_type=F32) + bias
            sink_a, sink_b = sink_ref[head_a], sink_ref[head_a + 1]
            m_a = jnp.maximum(jnp.max(s[:, :nk], axis=-1, keepdims=True), sink_a)
            m_b = jnp.maximum(jnp.max(s[:, nk:], axis=-1, keepdims=True), sink_b)
            p = jnp.concatenate([jnp.exp(s[:, :nk] - m_a), jnp.exp(s[:, nk:] - m_b)], axis=1).astype(BF16)
            r = _dot(p, vobd)
            sink_term = jnp.where(out_lane < hd, jnp.exp(sink_a - m_a), jnp.exp(sink_b - m_b))
            o_ref[:, col0:col0 + lanes] = (r[:, :lanes] / (r[:, lanes:] + sink_term)).astype(BF16)


def _swa(q, k, v, cache_k, cache_v, sinks, *, bp, sp, bs, ss, tq):
    t, d = q.shape
    kvw = k.shape[1]
    tp = bp * sp
    n_heads = sinks.shape[0]
    hd = d // n_heads
    n_kv = kvw // hd
    back = N_BACK * CHUNK
    assert tq == back and sp % tq == 0 and ss == CHUNK and cache_k.shape[1] == back
    smem = pl.BlockSpec(memory_space=pltpu.SMEM)
    nj = sp // tq
    kw = dict(n_kv=n_kv, group=n_heads // n_kv, hd=hd)
    prev_blk = pl.BlockSpec((tq, kvw), lambda b, j: (b * nj + jnp.maximum(j - 1, 0), 0))
    this_blk = pl.BlockSpec((tq, kvw), lambda b, j: (b * nj + j, 0))
    o_p = pl.pallas_call(
        functools.partial(_swa_kernel, prompt=True, tq=tq, **kw),
        grid=(bp, nj),
        in_specs=[smem, pl.BlockSpec((tq, d), lambda b, j: (b * nj + j, 0)), prev_blk, this_blk, prev_blk, this_blk],
        out_specs=pl.BlockSpec((tq, d), lambda b, j: (b * nj + j, 0)),
        out_shape=jax.ShapeDtypeStruct((tp, d), BF16),
        compiler_params=_params(2),
        name="swa_prompt",
    )(sinks, q, k, k, v, v)
    off = tp // ss
    cache_blk = pl.BlockSpec((1, back, kvw), lambda b: (b, 0, 0))
    new_blk = pl.BlockSpec((ss, kvw), lambda b: (off + b, 0))
    o_s = pl.pallas_call(
        functools.partial(_swa_kernel, prompt=False, tq=ss, **kw),
        grid=(bs,),
        in_specs=[smem, pl.BlockSpec((ss, d), lambda b: (off + b, 0)), cache_blk, new_blk, cache_blk, new_blk],
        out_specs=pl.BlockSpec((ss, d), lambda b: (b, 0)),
        out_shape=jax.ShapeDtypeStruct((bs * ss, d), BF16),
        compiler_params=_params(1),
        name="swa_sample",
    )(sinks, q, cache_k.reshape(bs, back, kvw), k, cache_v.reshape(bs, back, kvw), v)
    return o_p, o_s


def _oproj_route_kernel(op_ref, os_ref, x3_ref, wo_ref, gffn_ref, wr_ref, br_ref, x4_ref, hm_ref, route_ref,
                        *, n_prompt_tiles):
    o = jnp.where(pl.program_id(0) < n_prompt_tiles, op_ref[...], os_ref[...])
    x4 = x3_ref[...] + _dot(o, wo_ref[...])
    x4_ref[...] = x4
    hm = _rms(x4, gffn_ref[...])
    hm_ref[...] = hm
    hm_hi = hm.astype(BF16)
    hm_lo = (hm - hm_hi.astype(F32)).astype(BF16)
    wr = wr_ref[...]
    wr_hi = wr.astype(BF16)
    wr_lo = (wr - wr_hi.astype(F32)).astype(BF16)
    logits = (_dot(hm_hi, wr_hi) + (_dot(hm_lo, wr_hi) + _dot(hm_hi, wr_lo))) + br_ref[...]
    lane = lax.broadcasted_iota(jnp.int32, logits.shape, 1).astype(F32)
    m1 = jnp.max(logits, axis=-1, keepdims=True)
    i1 = jnp.min(jnp.where(logits == m1, lane, float(ROUTE_LANES)), axis=-1, keepdims=True)
    rest = jnp.where(lane == i1, -jnp.inf, logits)
    m2 = jnp.max(rest, axis=-1, keepdims=True)
    i2 = jnp.min(jnp.where(rest == m2, lane, float(ROUTE_LANES)), axis=-1, keepdims=True)
    e2 = jnp.exp(m2 - m1)
    g1 = 1.0 / (1.0 + e2)
    g2 = e2 / (1.0 + e2)
    route = jnp.where(lane == 0.0, i1, jnp.where(lane == 1.0, i2,
                      jnp.where(lane == 2.0, g1, jnp.where(lane == 3.0, g2, 0.0))))
    route_ref[...] = route


def _oproj_route(o_p, o_s, x3, wo, gffn, wr, br, *, tm):
    t, d = x3.shape
    tp = o_p.shape[0]
    assert tp % tm == 0 and o_s.shape[0] % tm == 0
    row = lambda w: pl.BlockSpec((tm, w), lambda i: (i, 0))
    return pl.pallas_call(
        functools.partial(_oproj_route_kernel, n_prompt_tiles=tp // tm),
        grid=(t // tm,),
        in_specs=_split_specs(tm, d, tp // tm) + [row(d), _const_spec(wo.shape), pl.BlockSpec((1, d), lambda i: (0, 0)),
                  _const_spec(wr.shape), pl.BlockSpec((1, ROUTE_LANES), lambda i: (0, 0))],
        out_specs=[row(d), row(d), row(ROUTE_LANES)],
        out_shape=[jax.ShapeDtypeStruct((t, d), F32), jax.ShapeDtypeStruct((t, d), F32),
                   jax.ShapeDtypeStruct((t, ROUTE_LANES), F32)],
        compiler_params=_params(1),
        name="oproj_route",
    )(o_p, o_s, x3, wo, gffn, wr, br)


def _moe_kernel(texp_ref, rtok_ref, rdst_ref, hm_ref, w1_ref, w3_ref, w2_ref, y_ref,
                xbuf0, xbuf1, obuf0, obuf1, acc_ref, gsem, ssem, *, tm, fc, n_steps):
    i = pl.program_id(0)

    def gather_row(base, r, xb, sem, priority=0):
        tok = rtok_ref[base + r]
        pltpu.make_async_copy(hm_ref.at[pl.ds(tok, 1), :], xb.at[pl.ds(r, 1), :], sem).start(priority)

    def scatter_row(base, r, ob, sem, priority=0):
        dst = rdst_ref[base + r]
        pltpu.make_async_copy(ob.at[pl.ds(r, 1), :], y_ref.at[pl.ds(dst, 1), :], sem).start(priority)

    def wait_gather(xb, sem):
        pltpu.make_async_copy(hm_ref.at[pl.ds(0, tm), :], xb, sem).wait()

    def wait_scatter(ob, sem):
        pltpu.make_async_copy(ob, y_ref.at[pl.ds(0, tm), :], sem).wait()

    @pl.when(i == 0)
    def _():
        obuf0[...] = jnp.zeros(obuf0.shape, F32)
        obuf1[...] = jnp.zeros(obuf1.shape, F32)
        lax.fori_loop(0, tm, lambda r, c: (gather_row(0, r, xbuf0, gsem.at[0]), c)[1], 0)

    def step(xb, ob, xb_next, ob_prev, p):
        wait_gather(xb, gsem.at[p])
        acc_ref[...] = jnp.zeros(acc_ref.shape, F32)
        g_base = (i + 1) * tm
        s_base = i * tm

        def move_rows(c, n_chunks):
            per = -(-2 * tm // n_chunks)
            for m in range(c * per, min((c + 1) * per, 2 * tm)):
                if m < tm:
                    gather_row(g_base, m, xb_next, gsem.at[1 - p], m % 2)
                else:
                    scatter_row(s_base, m - tm, ob_prev, ssem.at[1 - p], m % 2)

        _swiglu_acc(xb[...].astype(BF16), w1_ref, w3_ref, w2_ref, acc_ref, fc, after_chunk=move_rows)

        @pl.when(i >= 1)
        def _():
            wait_scatter(ob, ssem.at[p])
        ob[...] = acc_ref[...]

    @pl.when(i % 2 == 0)
    def _():
        step(xbuf0, obuf0, xbuf1, obuf1, 0)

    @pl.when(i % 2 == 1)
    def _():
        step(xbuf1, obuf1, xbuf0, obuf0, 1)

    @pl.when(i == n_steps - 1)
    def _():
        p = (n_steps - 1) % 2
        xb_next, ob_prev = (xbuf1, obuf1) if p == 0 else (xbuf0, obuf0)
        wait_gather(xb_next, gsem.at[1 - p])
        wait_scatter(ob_prev, ssem.at[1 - p])


def _moe(hm, route, w1, w3, w2, *, tm, fc):
    t, d = hm.shape
    n_exp = w1.shape[0]
    n_assign = t * TOP_K
    nt = n_assign // tm + n_exp
    assert n_assign % tm == 0
    e_flat = route[:, :TOP_K].astype(jnp.int32).reshape(-1)
    order = jnp.argsort(e_flat, stable=True).astype(jnp.int32)
    counts = jnp.sum((e_flat[:, None] == jnp.arange(n_exp, dtype=jnp.int32)[None, :]).astype(jnp.int32), axis=0)
    tiles_e = (counts + tm - 1) // tm
    tile_end = jnp.cumsum(tiles_e)
    tile_start = tile_end - tiles_e
    cnt_start = jnp.cumsum(counts) - counts
    tile_ids = jnp.arange(nt, dtype=jnp.int32)
    texp = jnp.minimum(jnp.sum((tile_ids[:, None] >= tile_end[None, :]).astype(jnp.int32), axis=1), n_exp - 1)
    within = (tile_ids - tile_start[texp]) * tm
    nval = jnp.where(tile_ids < tile_end[-1], jnp.clip(counts[texp] - within, 0, tm), 0).astype(jnp.int32)
    r_in = jnp.arange(tm, dtype=jnp.int32)[None, :]
    src = jnp.clip(cnt_start[texp][:, None] + within[:, None] + r_in, 0, n_assign - 1)
    real = r_in < nval[:, None]
    assign = jnp.where(real, order[src], 0)
    scratch_row = TOP_K * t + (tile_ids[:, None] % 2) * tm + r_in
    rtok = jnp.concatenate([(assign // TOP_K).reshape(-1), jnp.zeros((2 * tm,), jnp.int32)])
    rdst = jnp.concatenate([TOP_K * t + tm + r_in[0],
                            jnp.where(real, (assign % TOP_K) * t + assign // TOP_K, scratch_row).reshape(-1)])
    texp = jnp.concatenate([texp, texp[-1:]])
    n_steps = nt + 1

    wspec = lambda shape: pl.BlockSpec((None,) + shape[1:], lambda i, te, *_: (te[i], 0, 0),
                                       pipeline_mode=pl.Buffered(1))
    y2 = pl.pallas_call(
        functools.partial(_moe_kernel, tm=tm, fc=fc, n_steps=n_steps),
        grid_spec=pltpu.PrefetchScalarGridSpec(
            num_scalar_prefetch=3,
            grid=(n_steps,),
            in_specs=[pl.BlockSpec(memory_space=pl.ANY), wspec(w1.shape), wspec(w3.shape), wspec(w2.shape)],
            out_specs=pl.BlockSpec(memory_space=pl.ANY),
            scratch_shapes=[pltpu.VMEM((tm, d), F32)] * 5
            + [pltpu.SemaphoreType.DMA((2,)), pltpu.SemaphoreType.DMA((2,))],
        ),
        out_shape=jax.ShapeDtypeStruct((TOP_K * t + 2 * tm, d), F32),
        compiler_params=_params(1),
        name="moe",
    )(texp, rtok, rdst, hm, w1, w3, w2)
    return y2


def _final_kernel(x4_ref, ya_ref, yb_ref, route_ref, pp_ref, ps_ref, gple_ref, wproj_ref, wgate_ref, gfin_ref,
                  yp_ref, ys_ref, *, n_prompt_tiles):
    i = pl.program_id(0)
    route = route_ref[...]
    g1 = route[:, 2:3]
    g2 = route[:, 3:4]
    x5 = x4_ref[...] + (g1 * ya_ref[...] + g2 * yb_ref[...])
    p = jnp.where(i < n_prompt_tiles, pp_ref[...], ps_ref[...]).astype(BF16)
    x6 = _ple_gate(x5, p, gple_ref[...], wproj_ref, wgate_ref)
    y = _rms(x6, gfin_ref[...])

    @pl.when(i < n_prompt_tiles)
    def _():
        yp_ref[...] = y

    @pl.when(i >= n_prompt_tiles)
    def _():
        ys_ref[...] = y


def _final(x4, y2, route, pp, ps, gple, wproj, wgate, gfin, *, tm):
    t, d = x4.shape
    tp, ts = pp.shape[1], ps.shape[1]
    npt = tp // tm
    assert tp % tm == 0 and ts % tm == 0
    row = lambda w: pl.BlockSpec((tm, w), lambda i: (i, 0))
    vec = pl.BlockSpec((1, d), lambda i: (0, 0))
    return pl.pallas_call(
        functools.partial(_final_kernel, n_prompt_tiles=npt),
        grid=(t // tm,),
        in_specs=[row(d), row(d), pl.BlockSpec((tm, d), lambda i: (t // tm + i, 0)), row(ROUTE_LANES)]
        + _split_specs(tm, pp.shape[2], npt, layer=1) + [vec, _const_spec(wproj.shape), _const_spec(wgate.shape), vec],
        out_specs=_split_specs(tm, d, npt),
        out_shape=[jax.ShapeDtypeStruct((tp, d), F32), jax.ShapeDtypeStruct((ts, d), F32)],
        compiler_params=_params(1),
        name="final",
    )(x4, y2, y2, route, pp, ps, gple, wproj, wgate, gfin)


def kernel(x_prompt, x_sample, state_pool, cache_k, cache_v, p_prompt, p_sample, g_mix, g_ffn, g_ple, g_kv, g_final,
           w_pool, pool_scale, w_q, b_q, w_k, b_k, w_v, b_v, w_o, sinks, w1_dense, w3_dense, w2_dense, w_router,
           b_router, w1_moe, w3_moe, w2_moe, w_ple_proj, w_ple_gate):
    bp, sp, d = x_prompt.shape
    bs, ss, _ = x_sample.shape
    depth = g_mix.shape[0]
    assert depth == 2 and w_pool.shape[0] == 1 and w_q.shape[0] == 1 and w1_dense.shape[0] == 1 and w1_moe.shape[0] == 1
    tp, ts = bp * sp, bs * ss
    n_kv, hd = cache_k.shape[2], cache_k.shape[3]
    n_exp = w_router.shape[2]
    ple = p_prompt.shape[-1]
    bf = lambda w: w.astype(BF16)
    vec = lambda g: g.reshape(1, -1)
    pp = p_prompt.reshape(depth, tp, ple)
    ps = p_sample.reshape(depth, ts, ple)

    x1_p, x1_s, hf_p, hf_s, pool_p, pool_s = _pool_front(
        x_prompt, x_sample, state_pool[0], vec(g_mix[0]), vec(g_ffn[0]), bf(w_pool[0]), vec(pool_scale[0]),
        ta=512, nbs=8)

    x3, k, v, q = _ffn0_post(
        hf_p, hf_s, x1_p, x1_s, pp, ps,
        bf(w1_dense[0]), bf(w3_dense[0]), bf(w2_dense[0]),
        vec(g_ple[0]), bf(w_ple_proj[0]), bf(w_ple_gate[0]),
        vec(g_kv), bf(w_k), vec(b_k), bf(w_v), vec(b_v),
        vec(g_mix[1]), bf(w_q[0]), vec(b_q[0]), tm=512, fc=512)

    o_p, o_s = _swa(q, k, v, cache_k, cache_v, sinks[0], bp=bp, sp=sp, bs=bs, ss=ss, tq=N_BACK * CHUNK)

    wr = jnp.pad(w_router[0], ((0, 0), (0, ROUTE_LANES - n_exp)))
    br = jnp.pad(b_router[0], (0, ROUTE_LANES - n_exp), constant_values=NEG).reshape(1, ROUTE_LANES)
    x4, hm, route = _oproj_route(o_p, o_s, x3, bf(w_o[0]), vec(g_ffn[1]), wr, br, tm=512)

    y2 = _moe(hm, route, bf(w1_moe[0]), bf(w3_moe[0]), bf(w2_moe[0]), tm=512, fc=512)

    y_p, y_s = _final(x4, y2, route, pp, ps,
                      vec(g_ple[1]), bf(w_ple_proj[1]), bf(w_ple_gate[1]), vec(g_final), tm=512)

    back = N_BACK * CHUNK

    def new_cache(rows, cache):
        b = rows.shape[0] // (sp if cache is None else ss)
        rows = rows.reshape(b, -1, n_kv, hd)
        if cache is not None:
            rows = jnp.concatenate([cache, rows], axis=1)
        return rows[:, -back:]

    return (y_p.reshape(bp, sp, d), y_s.reshape(bs, ss, d), pool_p, pool_s,
            new_cache(k[:tp], None), new_cache(v[:tp], None), new_cache(k[tp:], cache_k), new_cache(v[tp:], cache_v))
```

```python
import functools

import jax
import jax.numpy as jnp
from jax import lax
from jax.experimental import pallas as pl
from jax.experimental.pallas import tpu as pltpu

EPS = 1e-6
NEG = -1e30
CHUNK = 64
N_BACK = 2
POOL_WINDOWS = (2, 4, 8, 16)
POOL_BUF = 15
HIST = 16
TOP_K = 2
ROUTE_LANES = 128
VMEM_LIMIT = 56 * 1024 * 1024

F32 = jnp.float32
BF16 = jnp.bfloat16


def _rms(x, g):
    return (x * lax.rsqrt(jnp.mean(x * x, axis=-1, keepdims=True) + EPS)) * g


def _dot(a, b):
    return jnp.dot(a, b, preferred_element_type=F32)


def _const_spec(shape):
    nd = len(shape)
    return pl.BlockSpec(shape, lambda *_: (0,) * nd, pipeline_mode=pl.Buffered(1))


def _params(n_axes):
    return pltpu.CompilerParams(dimension_semantics=("arbitrary",) * n_axes, vmem_limit_bytes=VMEM_LIMIT)


def _pool_front_kernel(*refs, prompt, nb, ta, d):
    (x_ref, hist_ref, gmix_ref, gffn_ref, wpool_ref, scale_ref,
     x1_ref, hf_ref, state_ref, e_ref, f2_ref, f4_ref, f8_ref) = refs
    gw = d // len(POOL_WINDOWS)
    j = pl.program_id(1)
    n = ta + HIST
    x = x_ref[...]
    g = gmix_ref[...]
    h = _rms(x, g)
    if prompt:
        hp = _rms(hist_ref[...], g)
        hp = jnp.where(j == 0, 0.0, hp)
        e_ref[:, 0:HIST, :] = hp
    else:
        e_ref[:, 0:1, :] = jnp.zeros((nb, 1, d), F32)
        e_ref[:, 1:HIST, :] = hist_ref[...]
    e_ref[:, HIST:n, :] = h
    zero_tail = jnp.zeros((nb, 8, d), F32)
    e_ref[:, n:n + 8, :] = zero_tail
    f2_ref[:, 0:n, :] = e_ref[:, 0:n, :] + e_ref[:, 1:n + 1, :]
    f2_ref[:, n:n + 8, :] = zero_tail
    f4_ref[:, 0:n, :] = f2_ref[:, 0:n, gw:] + f2_ref[:, 2:n + 2, gw:]
    f4_ref[:, n:n + 8, :] = zero_tail[:, :, gw:]
    f8_ref[:, 0:n, :] = f4_ref[:, 0:n, gw:] + f4_ref[:, 4:n + 4, gw:]
    f8_ref[:, n:n + 8, :] = zero_tail[:, :, 2 * gw:]
    sums = (
        f2_ref[:, HIST - 1:HIST - 1 + ta, 0:gw],
        f4_ref[:, HIST - 3:HIST - 3 + ta, 0:gw],
        f8_ref[:, HIST - 7:HIST - 7 + ta, 0:gw],
        f8_ref[:, HIST - 15:HIST - 15 + ta, gw:] + f8_ref[:, HIST - 7:HIST - 7 + ta, gw:],
    )
    t_idx = lax.broadcasted_iota(jnp.int32, (1, ta, 1), 1) + j * ta
    ys = []
    for gi, w in enumerate(POOL_WINDOWS):
        if prompt:
            cnt = jnp.minimum(t_idx + 1, w).astype(F32)
            mean = sums[gi] / cnt
        else:
            mean = sums[gi] * (1.0 / w)
        diff = mean - h[:, :, gi * gw:(gi + 1) * gw]
        ys.append(_dot(diff.reshape(nb * ta, gw).astype(BF16), wpool_ref[gi]))
    y = jnp.concatenate(ys, axis=-1) * scale_ref[...]
    x1 = x.reshape(nb * ta, d) + y
    x1_ref[...] = x1
    hf_ref[...] = _rms(x1, gffn_ref[...]).astype(BF16)
    if prompt:
        @pl.when(j == pl.num_programs(1) - 1)
        def _():
            state_ref[0] = e_ref[:, n - POOL_BUF:n, :]
    else:
        state_ref[0] = e_ref[:, n - POOL_BUF:n, :]


def _pool_front(x_p, x_s, state, gmix, gffn, wpool, scale, *, ta, nbs):
    bp, sp, d = x_p.shape
    bs, ss, _ = x_s.shape
    tp, ts = bp * sp, bs * ss
    gw = d // len(POOL_WINDOWS)
    assert sp % ta == 0 and ta % HIST == 0 and bs % nbs == 0
    vec = pl.BlockSpec((1, d), lambda *_: (0, 0))
    wspec = pl.BlockSpec((len(POOL_WINDOWS), gw, gw), lambda *_: (0, 0, 0))

    def scratch(nb, rows):
        return [pltpu.VMEM((nb, rows + HIST + 8, d), F32), pltpu.VMEM((nb, rows + HIST + 8, d), F32),
                pltpu.VMEM((nb, rows + HIST + 8, d - gw), F32), pltpu.VMEM((nb, rows + HIST + 8, d - 2 * gw), F32)]

    njp = sp // ta
    x1_p, hf_p, state_p = pl.pallas_call(
        functools.partial(_pool_front_kernel, prompt=True, nb=1, ta=ta, d=d),
        grid=(bp, njp),
        in_specs=[
            pl.BlockSpec((1, ta, d), lambda b, j: (b, j, 0)),
            pl.BlockSpec((1, HIST, d), lambda b, j: (b, jnp.maximum(j * (ta // HIST) - 1, 0), 0)),
            vec, vec, wspec, vec,
        ],
        out_specs=[
            pl.BlockSpec((ta, d), lambda b, j: (b * njp + j, 0)),
            pl.BlockSpec((ta, d), lambda b, j: (b * njp + j, 0)),
            pl.BlockSpec((1, 1, POOL_BUF, d), lambda b, j: (0, b, 0, 0)),
        ],
        out_shape=[jax.ShapeDtypeStruct((tp, d), F32), jax.ShapeDtypeStruct((tp, d), BF16),
                   jax.ShapeDtypeStruct((1, bp, POOL_BUF, d), F32)],
        scratch_shapes=scratch(1, ta),
        compiler_params=_params(2),
        name="pool_front_prompt",
    )(x_p, x_p, gmix, gffn, wpool, scale)

    rows = nbs * ss
    x1_s, hf_s, state_s = pl.pallas_call(
        functools.partial(_pool_front_kernel, prompt=False, nb=nbs, ta=ss, d=d),
        grid=(bs // nbs, 1),
        in_specs=[
            pl.BlockSpec((nbs, ss, d), lambda b, j: (b, 0, 0)),
            pl.BlockSpec((nbs, POOL_BUF, d), lambda b, j: (b, 0, 0)),
            vec, vec, wspec, vec,
        ],
        out_specs=[
            pl.BlockSpec((rows, d), lambda b, j: (b, 0)),
            pl.BlockSpec((rows, d), lambda b, j: (b, 0)),
            pl.BlockSpec((1, nbs, POOL_BUF, d), lambda b, j: (0, b, 0, 0)),
        ],
        out_shape=[jax.ShapeDtypeStruct((ts, d), F32), jax.ShapeDtypeStruct((ts, d), BF16),
                   jax.ShapeDtypeStruct((1, bs, POOL_BUF, d), F32)],
        scratch_shapes=scratch(nbs, ss),
        compiler_params=_params(2),
        name="pool_front_sample",
    )(x_s, state, gmix, gffn, wpool, scale)
    return x1_p, x1_s, hf_p, hf_s, state_p, state_s


def _swiglu_acc(x_bf, w1_ref, w3_ref, w2_ref, acc_ref, fc, before_chunk=None, after_chunk=None):
    d_ff = w1_ref.shape[-1]
    n_chunks = d_ff // fc
    for c in range(n_chunks):
        if before_chunk is not None:
            before_chunk(c, n_chunks)
        sl = slice(c * fc, (c + 1) * fc)
        h1 = _dot(x_bf, w1_ref[:, sl])
        h3 = _dot(x_bf, w3_ref[:, sl])
        a = (h1 * jax.nn.sigmoid(h1)) * h3
        acc_ref[...] += _dot(a.astype(BF16), w2_ref[sl, :])
        if after_chunk is not None:
            after_chunk(c, n_chunks)


def _ple_gate(x, p_bf, gple, wproj_ref, wgate_ref):
    e = _dot(p_bf, wproj_ref[...])
    gate = jax.nn.sigmoid(_dot(_rms(x, gple).astype(BF16), wgate_ref[...]))
    return x + gate * e


def _ffn0_post_kernel(hfp_ref, hfs_ref, x1p_ref, x1s_ref, pp_ref, ps_ref, w1_ref, w3_ref, w2_ref,
                      gple_ref, wproj_ref, wgate_ref, gkv_ref, wk_ref, bk_ref, wv_ref, bv_ref,
                      gmix_ref, wq_ref, bq_ref,
                      x3_ref, k_ref, v_ref, q_ref, acc_ref, *, fc, n_prompt_tiles):
    is_prompt = pl.program_id(0) < n_prompt_tiles
    acc_ref[...] = jnp.where(is_prompt, x1p_ref[...], x1s_ref[...])
    _swiglu_acc(jnp.where(is_prompt, hfp_ref[...], hfs_ref[...]), w1_ref, w3_ref, w2_ref, acc_ref, fc)
    p = jnp.where(is_prompt, pp_ref[...], ps_ref[...]).astype(BF16)
    x3 = _ple_gate(acc_ref[...], p, gple_ref[...], wproj_ref, wgate_ref)
    x3_ref[...] = x3
    kv_in = _rms(x3, gkv_ref[...]).astype(BF16)
    k_ref[...] = _dot(kv_in, wk_ref[...]) + bk_ref[...]
    v_ref[...] = _dot(kv_in, wv_ref[...]) + bv_ref[...]
    hq = _rms(x3, gmix_ref[...]).astype(BF16)
    q_ref[...] = (_dot(hq, wq_ref[...]) + bq_ref[...]).astype(BF16)


def _split_specs(tm, width, n_prompt_tiles, layer=None):
    if layer is None:
        return [pl.BlockSpec((tm, width), lambda i, *_: (jnp.minimum(i, n_prompt_tiles - 1), 0)),
                pl.BlockSpec((tm, width), lambda i, *_: (jnp.maximum(i - n_prompt_tiles, 0), 0))]
    return [pl.BlockSpec((None, tm, width), lambda i, *_: (layer, jnp.minimum(i, n_prompt_tiles - 1), 0)),
            pl.BlockSpec((None, tm, width), lambda i, *_: (layer, jnp.maximum(i - n_prompt_tiles, 0), 0))]


def _ffn0_post(hf_p, hf_s, x1_p, x1_s, pp, ps, w1, w3, w2, gple, wproj, wgate, gkv, wk, bk, wv, bv, gmix, wq, bq,
               *, tm, fc):
    tp, d = x1_p.shape
    ts = x1_s.shape[0]
    t = tp + ts
    kvw = wk.shape[1]
    assert ts % tm == 0 and tp % tm == 0 and w1.shape[1] % fc == 0
    npt = tp // tm
    row = lambda w: pl.BlockSpec((tm, w), lambda i: (i, 0))
    vec = lambda w: pl.BlockSpec((1, w), lambda i: (0, 0))
    return pl.pallas_call(
        functools.partial(_ffn0_post_kernel, fc=fc, n_prompt_tiles=npt),
        grid=(t // tm,),
        in_specs=_split_specs(tm, d, npt) + _split_specs(tm, d, npt) + _split_specs(tm, pp.shape[2], npt, layer=0) + [
            _const_spec(w1.shape), _const_spec(w3.shape), _const_spec(w2.shape),
            vec(d), _const_spec(wproj.shape), _const_spec(wgate.shape),
            vec(d), _const_spec(wk.shape), vec(kvw), _const_spec(wv.shape), vec(kvw),
            vec(d), _const_spec(wq.shape), vec(d),
        ],
        out_specs=[row(d), row(kvw), row(kvw), row(d)],
        out_shape=[jax.ShapeDtypeStruct((t, d), F32), jax.ShapeDtypeStruct((t, kvw), F32),
                   jax.ShapeDtypeStruct((t, kvw), F32), jax.ShapeDtypeStruct((t, d), BF16)],
        scratch_shapes=[pltpu.VMEM((tm, d), F32)],
        compiler_params=_params(1),
        name="ffn0_post",
    )(hf_p, hf_s, x1_p, x1_s, pp, ps, w1, w3, w2, gple, wproj, wgate, gkv, wk, bk, wv, bv, gmix, wq, bq)


def _swa_kernel(*refs, prompt, tq, n_kv, group, hd):
    if prompt:
        sink_ref, q_ref, ka_ref, kb_ref, va_ref, vb_ref, o_ref = refs
        kk = jnp.concatenate([ka_ref[...], kb_ref[...]], axis=0)
        vv = jnp.concatenate([va_ref[...], vb_ref[...]], axis=0)
        first_chunk = N_BACK - pl.program_id(1) * (tq // CHUNK)
    else:
        sink_ref, q_ref, ck_ref, kn_ref, cv_ref, vn_ref, o_ref = refs
        pad = jnp.zeros(kn_ref.shape, F32)
        kk = jnp.concatenate([ck_ref[0], kn_ref[...], pad], axis=0)
        vv = jnp.concatenate([cv_ref[0], vn_ref[...], pad], axis=0)
        first_chunk = 0
    nk = kk.shape[0]
    lanes = 128
    per = lanes // hd
    assert per == 2 and group % per == 0 and nk % lanes == 0
    qc = lax.broadcasted_iota(jnp.int32, (tq, per * nk), 0) // CHUNK
    kc = (lax.broadcasted_iota(jnp.int32, (tq, per * nk), 1) % nk) // CHUNK
    bias = jnp.where(kc >= jnp.maximum(qc, first_chunk), jnp.where(kc <= qc + N_BACK, 0.0, NEG), NEG)
    lane = lax.broadcasted_iota(jnp.int32, (nk, lanes), 1)
    ones_a = jnp.where(lane < hd, 1.0, 0.0)
    ones_b = jnp.where(lane >= hd, 1.0, 0.0)
    out_lane = lax.broadcasted_iota(jnp.int32, (tq, lanes), 1)
    for kvh in range(n_kv):
        tile, half = divmod(kvh, per)
        keep = (lane < hd) if half == 0 else (lane >= hd)

        def both_halves(x):
            own = jnp.where(keep, x[:, tile * lanes:(tile + 1) * lanes], 0.0)
            other = pltpu.roll(own, hd, axis=1)
            return (own, other) if half == 0 else (other, own)

        k_a, k_b = both_halves(kk * (hd ** -0.5))
        v_a, v_b = both_halves(vv)
        kbd = jnp.concatenate([k_a, k_b], axis=0).astype(BF16)
        vobd = jnp.concatenate([jnp.concatenate([v_a, ones_a], axis=1),
                                jnp.concatenate([v_b, ones_b], axis=1)], axis=0).astype(BF16)
        for pr in range(group // per):
            head_a = kvh * group + pr * per
            col0 = head_a * hd
            s = lax.dot_general(q_ref[:, col0:col0 + lanes], kbd, (((1,), (1,)), ((), ())),
                                preferred_element_type=F32) + bias
            sink_a, sink_b = sink_ref[head_a], sink_ref[head_a + 1]
            m_a = jnp.maximum(jnp.max(s[:, :nk], axis=-1, keepdims=True), sink_a)
            m_b = jnp.maximum(jnp.max(s[:, nk:], axis=-1, keepdims=True), sink_b)
            p = jnp.concatenate([jnp.exp(s[:, :nk] - m_a), jnp.exp(s[:, nk:] - m_b)], axis=1).astype(BF16)
            r = _dot(p, vobd)
            sink_term = jnp.where(out_lane < hd, jnp.exp(sink_a - m_a), jnp.exp(sink_b - m_b))
            o_ref[:, col0:col0 + lanes] = (r[:, :lanes] / (r[:, lanes:] + sink_term)).astype(BF16)


def _swa(q, k, v, cache_k, cache_v, sinks, *, bp, sp, bs, ss, tq):
    t, d = q.shape
    kvw = k.shape[1]
    tp = bp * sp
    n_heads = sinks.shape[0]
    hd = d // n_heads
    n_kv = kvw // hd
    back = N_BACK * CHUNK
    assert tq == back and sp % tq == 0 and ss == CHUNK and cache_k.shape[1] == back
    smem = pl.BlockSpec(memory_space=pltpu.SMEM)
    nj = sp // tq
    kw = dict(n_kv=n_kv, group=n_heads // n_kv, hd=hd)
    prev_blk = pl.BlockSpec((tq, kvw), lambda b, j: (b * nj + jnp.maximum(j - 1, 0), 0))
    this_blk = pl.BlockSpec((tq, kvw), lambda b, j: (b * nj + j, 0))
    o_p = pl.pallas_call(
        functools.partial(_swa_kernel, prompt=True, tq=tq, **kw),
        grid=(bp, nj),
        in_specs=[smem, pl.BlockSpec((tq, d), lambda b, j: (b * nj + j, 0)), prev_blk, this_blk, prev_blk, this_blk],
        out_specs=pl.BlockSpec((tq, d), lambda b, j: (b * nj + j, 0)),
        out_shape=jax.ShapeDtypeStruct((tp, d), BF16),
        compiler_params=_params(2),
        name="swa_prompt",
    )(sinks, q, k, k, v, v)
    off = tp // ss
    cache_blk = pl.BlockSpec((1, back, kvw), lambda b: (b, 0, 0))
    new_blk = pl.BlockSpec((ss, kvw), lambda b: (off + b, 0))
    o_s = pl.pallas_call(
        functools.partial(_swa_kernel, prompt=False, tq=ss, **kw),
        grid=(bs,),
        in_specs=[smem, pl.BlockSpec((ss, d), lambda b: (off + b, 0)), cache_blk, new_blk, cache_blk, new_blk],
        out_specs=pl.BlockSpec((ss, d), lambda b: (b, 0)),
        out_shape=jax.ShapeDtypeStruct((bs * ss, d), BF16),
        compiler_params=_params(1),
        name="swa_sample",
    )(sinks, q, cache_k.reshape(bs, back, kvw), k, cache_v.reshape(bs, back, kvw), v)
    return o_p, o_s


def _oproj_route_kernel(op_ref, os_ref, x3_ref, wo_ref, gffn_ref, wr_ref, br_ref, x4_ref, hm_ref, route_ref,
                        *, n_prompt_tiles):
    o = jnp.where(pl.program_id(0) < n_prompt_tiles, op_ref[...], os_ref[...])
    x4 = x3_ref[...] + _dot(o, wo_ref[...])
    x4_ref[...] = x4
    hm = _rms(x4, gffn_ref[...])
    hm_ref[...] = hm
    hm_hi = hm.astype(BF16)
    hm_lo = (hm - hm_hi.astype(F32)).astype(BF16)
    wr = wr_ref[...]
    wr_hi = wr.astype(BF16)
    wr_lo = (wr - wr_hi.astype(F32)).astype(BF16)
    logits = (_dot(hm_hi, wr_hi) + (_dot(hm_lo, wr_hi) + _dot(hm_hi, wr_lo))) + br_ref[...]
    lane = lax.broadcasted_iota(jnp.int32, logits.shape, 1).astype(F32)
    m1 = jnp.max(logits, axis=-1, keepdims=True)
    i1 = jnp.min(jnp.where(logits == m1, lane, float(ROUTE_LANES)), axis=-1, keepdims=True)
    rest = jnp.where(lane == i1, -jnp.inf, logits)
    m2 = jnp.max(rest, axis=-1, keepdims=True)
    i2 = jnp.min(jnp.where(rest == m2, lane, float(ROUTE_LANES)), axis=-1, keepdims=True)
    e2 = jnp.exp(m2 - m1)
    g1 = 1.0 / (1.0 + e2)
    g2 = e2 / (1.0 + e2)
    route = jnp.where(lane == 0.0, i1, jnp.where(lane == 1.0, i2,
                      jnp.where(lane == 2.0, g1, jnp.where(lane == 3.0, g2, 0.0))))
    route_ref[...] = route


def _oproj_route(o_p, o_s, x3, wo, gffn, wr, br, *, tm):
    t, d = x3.shape
    tp = o_p.shape[0]
    assert tp % tm == 0 and o_s.shape[0] % tm == 0
    row = lambda w: pl.BlockSpec((tm, w), lambda i: (i, 0))
    return pl.pallas_call(
        functools.partial(_oproj_route_kernel, n_prompt_tiles=tp // tm),
        grid=(t // tm,),
        in_specs=_split_specs(tm, d, tp // tm) + [row(d), _const_spec(wo.shape), pl.BlockSpec((1, d), lambda i: (0, 0)),
                  _const_spec(wr.shape), pl.BlockSpec((1, ROUTE_LANES), lambda i: (0, 0))],
        out_specs=[row(d), row(d), row(ROUTE_LANES)],
        out_shape=[jax.ShapeDtypeStruct((t, d), F32), jax.ShapeDtypeStruct((t, d), F32),
                   jax.ShapeDtypeStruct((t, ROUTE_LANES), F32)],
        compiler_params=_params(1),
        name="oproj_route",
    )(o_p, o_s, x3, wo, gffn, wr, br)


def _moe_kernel(texp_ref, first_ref, rtok_ref, rdst_ref, hm_ref, w1_hbm, w3_hbm, w2_hbm, y_ref,
                xbuf0, xbuf1, obuf0, obuf1, acc_ref, w1_ref, w3_ref, w2_ref, st13, st2, gsem, ssem, wsem,
                *, tm, fc, n_steps):
    i = pl.program_id(0)
    n_slots = st2.shape[0]

    def gather_row(base, r, xb, sem):
        tok = rtok_ref[base + r]
        pltpu.make_async_copy(hm_ref.at[pl.ds(tok, 1), :], xb.at[pl.ds(r, 1), :], sem).start()

    def scatter_row(base, r, ob, sem):
        dst = rdst_ref[base + r]
        pltpu.make_async_copy(ob.at[pl.ds(r, 1), :], y_ref.at[pl.ds(dst, 1), :], sem).start()

    def weight_copies(c):
        e = texp_ref[i]
        slot = c % n_slots
        cols = pl.ds(c * fc, fc)
        return (pltpu.make_async_copy(w1_hbm.at[e, :, cols], st13.at[slot, 0], wsem.at[slot]),
                pltpu.make_async_copy(w3_hbm.at[e, :, cols], st13.at[slot, 1], wsem.at[slot]),
                pltpu.make_async_copy(w2_hbm.at[e, cols, :], st2.at[slot], wsem.at[slot]))

    def start_weights(c):
        for cp in weight_copies(c):
            cp.start(1)

    def stage_weights(c, n_chunks):
        for cp in weight_copies(c):
            cp.wait()
        slot = c % n_slots
        w1_ref[:, c * fc:(c + 1) * fc] = st13[slot, 0].astype(BF16)
        w3_ref[:, c * fc:(c + 1) * fc] = st13[slot, 1].astype(BF16)
        w2_ref[c * fc:(c + 1) * fc, :] = st2[slot].astype(BF16)
        if c + n_slots < n_chunks:
            start_weights(c + n_slots)

    def wait_gather(xb, sem):
        pltpu.make_async_copy(hm_ref.at[pl.ds(0, tm), :], xb, sem).wait()

    def wait_scatter(ob, sem):
        pltpu.make_async_copy(ob, y_ref.at[pl.ds(0, tm), :], sem).wait()

    @pl.when(i == 0)
    def _():
        obuf0[...] = jnp.zeros(obuf0.shape, F32)
        obuf1[...] = jnp.zeros(obuf1.shape, F32)
        lax.fori_loop(0, tm, lambda r, c: (gather_row(0, r, xbuf0, gsem.at[0]), c)[1], 0)

    is_first = first_ref[i] == 1

    @pl.when(is_first)
    def _():
        for c in range(n_slots):
            start_weights(c)

    def step(xb, ob, xb_next, ob_prev, p, streaming):
        wait_gather(xb, gsem.at[p])
        acc_ref[...] = jnp.zeros(acc_ref.shape, F32)
        g_base = (i + 1) * tm
        s_base = i * tm

        def move_rows(c, n_chunks):
            per = -(-2 * tm // n_chunks)
            for m in range(c * per, min((c + 1) * per, 2 * tm)):
                if m < tm:
                    gather_row(g_base, m, xb_next, gsem.at[1 - p])
                else:
                    scatter_row(s_base, m - tm, ob_prev, ssem.at[1 - p])

        _swiglu_acc(xb[...].astype(BF16), w1_ref, w3_ref, w2_ref, acc_ref, fc,
                    before_chunk=stage_weights if streaming else None, after_chunk=move_rows)

        @pl.when(i >= 1)
        def _():
            wait_scatter(ob, ssem.at[p])
        ob[...] = acc_ref[...]

    for p, bufs in enumerate(((xbuf0, obuf0, xbuf1, obuf1), (xbuf1, obuf1, xbuf0, obuf0))):
        for streaming in (False, True):
            @pl.when(jnp.logical_and(i % 2 == p, is_first == streaming))
            def _(p=p, bufs=bufs, streaming=streaming):
                step(*bufs, p, streaming)

    @pl.when(i == n_steps - 1)
    def _():
        p = (n_steps - 1) % 2
        xb_next, ob_prev = (xbuf1, obuf1) if p == 0 else (xbuf0, obuf0)
        wait_gather(xb_next, gsem.at[1 - p])
        wait_scatter(ob_prev, ssem.at[1 - p])


def _moe(hm, route, w1, w3, w2, *, tm, fc):
    t, d = hm.shape
    n_exp = w1.shape[0]
    n_assign = t * TOP_K
    nt = n_assign // tm + n_exp
    assert n_assign % tm == 0
    e_flat = route[:, :TOP_K].astype(jnp.int32).reshape(-1)
    order = jnp.argsort(e_flat, stable=True).astype(jnp.int32)
    counts = jnp.sum((e_flat[:, None] == jnp.arange(n_exp, dtype=jnp.int32)[None, :]).astype(jnp.int32), axis=0)
    tiles_e = (counts + tm - 1) // tm
    tile_end = jnp.cumsum(tiles_e)
    tile_start = tile_end - tiles_e
    cnt_start = jnp.cumsum(counts) - counts
    tile_ids = jnp.arange(nt, dtype=jnp.int32)
    texp = jnp.minimum(jnp.sum((tile_ids[:, None] >= tile_end[None, :]).astype(jnp.int32), axis=1), n_exp - 1)
    within = (tile_ids - tile_start[texp]) * tm
    nval = jnp.where(tile_ids < tile_end[-1], jnp.clip(counts[texp] - within, 0, tm), 0).astype(jnp.int32)
    r_in = jnp.arange(tm, dtype=jnp.int32)[None, :]
    src = jnp.clip(cnt_start[texp][:, None] + within[:, None] + r_in, 0, n_assign - 1)
    real = r_in < nval[:, None]
    assign = jnp.where(real, order[src], 0)
    scratch_row = TOP_K * t + (tile_ids[:, None] % 2) * tm + r_in
    rtok = jnp.concatenate([(assign // TOP_K).reshape(-1), jnp.zeros((2 * tm,), jnp.int32)])
    rdst = jnp.concatenate([TOP_K * t + tm + r_in[0],
                            jnp.where(real, (assign % TOP_K) * t + assign // TOP_K, scratch_row).reshape(-1)])
    texp = jnp.concatenate([texp, texp[-1:]])
    prev_exp = jnp.concatenate([jnp.full((1,), -1, jnp.int32), texp[:-1]])
    first = jnp.where(jnp.arange(nt + 1) < tile_end[-1], texp != prev_exp, False).astype(jnp.int32)
    n_steps = nt + 1
    d_ff = w1.shape[2]
    n_slots = 2
    assert d_ff % fc == 0 and d_ff // fc >= n_slots

    hbm = pl.BlockSpec(memory_space=pl.ANY)
    y2 = pl.pallas_call(
        functools.partial(_moe_kernel, tm=tm, fc=fc, n_steps=n_steps),
        grid_spec=pltpu.PrefetchScalarGridSpec(
            num_scalar_prefetch=4,
            grid=(n_steps,),
            in_specs=[hbm, hbm, hbm, hbm],
            out_specs=hbm,
            scratch_shapes=[pltpu.VMEM((tm, d), F32)] * 5
            + [pltpu.VMEM((d, d_ff), BF16), pltpu.VMEM((d, d_ff), BF16), pltpu.VMEM((d_ff, d), BF16),
               pltpu.VMEM((n_slots, 2, d, fc), F32), pltpu.VMEM((n_slots, fc, d), F32),
               pltpu.SemaphoreType.DMA((2,)), pltpu.SemaphoreType.DMA((2,)), pltpu.SemaphoreType.DMA((n_slots,))],
        ),
        out_shape=jax.ShapeDtypeStruct((TOP_K * t + 2 * tm, d), F32),
        compiler_params=_params(1),
        name="moe",
    )(texp, first, rtok, rdst, hm, w1, w3, w2)
    return y2


def _final_kernel(x4_ref, ya_ref, yb_ref, route_ref, pp_ref, ps_ref, gple_ref, wproj_ref, wgate_ref, gfin_ref,
                  yp_ref, ys_ref, *, n_prompt_tiles):
    i = pl.program_id(0)
    route = route_ref[...]
    g1 = route[:, 2:3]
    g2 = route[:, 3:4]
    x5 = x4_ref[...] + (g1 * ya_ref[...] + g2 * yb_ref[...])
    p = jnp.where(i < n_prompt_tiles, pp_ref[...], ps_ref[...]).astype(BF16)
    x6 = _ple_gate(x5, p, gple_ref[...], wproj_ref, wgate_ref)
    y = _rms(x6, gfin_ref[...])

    @pl.when(i < n_prompt_tiles)
    def _():
        yp_ref[...] = y

    @pl.when(i >= n_prompt_tiles)
    def _():
        ys_ref[...] = y


def _final(x4, y2, route, pp, ps, gple, wproj, wgate, gfin, *, tm):
    t, d = x4.shape
    tp, ts = pp.shape[1], ps.shape[1]
    npt = tp // tm
    assert tp % tm == 0 and ts % tm == 0
    row = lambda w: pl.BlockSpec((tm, w), lambda i: (i, 0))
    vec = pl.BlockSpec((1, d), lambda i: (0, 0))
    return pl.pallas_call(
        functools.partial(_final_kernel, n_prompt_tiles=npt),
        grid=(t // tm,),
        in_specs=[row(d), row(d), pl.BlockSpec((tm, d), lambda i: (t // tm + i, 0)), row(ROUTE_LANES)]
        + _split_specs(tm, pp.shape[2], npt, layer=1) + [vec, _const_spec(wproj.shape), _const_spec(wgate.shape), vec],
        out_specs=_split_specs(tm, d, npt),
        out_shape=[jax.ShapeDtypeStruct((tp, d), F32), jax.ShapeDtypeStruct((ts, d), F32)],
        compiler_params=_params(1),
        name="final",
    )(x4, y2, y2, route, pp, ps, gple, wproj, wgate, gfin)


def kernel(x_prompt, x_sample, state_pool, cache_k, cache_v, p_prompt, p_sample, g_mix, g_ffn, g_ple, g_kv, g_final,
           w_pool, pool_scale, w_q, b_q, w_k, b_k, w_v, b_v, w_o, sinks, w1_dense, w3_dense, w2_dense, w_router,
           b_router, w1_moe, w3_moe, w2_moe, w_ple_proj, w_ple_gate):
    bp, sp, d = x_prompt.shape
    bs, ss, _ = x_sample.shape
    depth = g_mix.shape[0]
    assert depth == 2 and w_pool.shape[0] == 1 and w_q.shape[0] == 1 and w1_dense.shape[0] == 1 and w1_moe.shape[0] == 1
    tp, ts = bp * sp, bs * ss
    n_kv, hd = cache_k.shape[2], cache_k.shape[3]
    n_exp = w_router.shape[2]
    ple = p_prompt.shape[-1]
    bf = lambda w: w.astype(BF16)
    vec = lambda g: g.reshape(1, -1)
    pp = p_prompt.reshape(depth, tp, ple)
    ps = p_sample.reshape(depth, ts, ple)

    x1_p, x1_s, hf_p, hf_s, pool_p, pool_s = _pool_front(
        x_prompt, x_sample, state_pool[0], vec(g_mix[0]), vec(g_ffn[0]), bf(w_pool[0]), vec(pool_scale[0]),
        ta=512, nbs=8)

    x3, k, v, q = _ffn0_post(
        hf_p, hf_s, x1_p, x1_s, pp, ps,
        bf(w1_dense[0]), bf(w3_dense[0]), bf(w2_dense[0]),
        vec(g_ple[0]), bf(w_ple_proj[0]), bf(w_ple_gate[0]),
        vec(g_kv), bf(w_k), vec(b_k), bf(w_v), vec(b_v),
        vec(g_mix[1]), bf(w_q[0]), vec(b_q[0]), tm=512, fc=512)

    o_p, o_s = _swa(q, k, v, cache_k, cache_v, sinks[0], bp=bp, sp=sp, bs=bs, ss=ss, tq=N_BACK * CHUNK)

    wr = jnp.pad(w_router[0], ((0, 0), (0, ROUTE_LANES - n_exp)))
    br = jnp.pad(b_router[0], (0, ROUTE_LANES - n_exp), constant_values=NEG).reshape(1, ROUTE_LANES)
    x4, hm, route = _oproj_route(o_p, o_s, x3, bf(w_o[0]), vec(g_ffn[1]), wr, br, tm=512)

    y2 = _moe(hm, route, w1_moe[0], w3_moe[0], w2_moe[0], tm=512, fc=512)

    y_p, y_s = _final(x4, y2, route, pp, ps,
                      vec(g_ple[1]), bf(w_ple_proj[1]), bf(w_ple_gate[1]), vec(g_final), tm=512)

    back = N_BACK * CHUNK

    def new_cache(rows, cache):
        b = rows.shape[0] // (sp if cache is None else ss)
        rows = rows.reshape(b, -1, n_kv, hd)
        if cache is not None:
            rows = jnp.concatenate([cache, rows], axis=1)
        return rows[:, -back:]

    return (y_p.reshape(bp, sp, d), y_s.reshape(bs, ss, d), pool_p, pool_s,
            new_cache(k[:tp], None), new_cache(v[:tp], None), new_cache(k[tp:], cache_k), new_cache(v[tp:], cache_v))
```

```python
import functools

import jax
import jax.numpy as jnp
from jax import lax
from jax.experimental import pallas as pl
from jax.experimental.pallas import tpu as pltpu

EPS = 1e-6
NEG = -1e30
CHUNK = 64
N_BACK = 2
POOL_WINDOWS = (2, 4, 8, 16)
POOL_BUF = 15
HIST = 16
TOP_K = 2
ROUTE_LANES = 128
VMEM_LIMIT = 56 * 1024 * 1024

F32 = jnp.float32
BF16 = jnp.bfloat16


def _rms(x, g):
    return (x * lax.rsqrt(jnp.mean(x * x, axis=-1, keepdims=True) + EPS)) * g


def _dot(a, b):
    return jnp.dot(a, b, preferred_element_type=F32)


def _const_spec(shape):
    nd = len(shape)
    return pl.BlockSpec(shape, lambda *_: (0,) * nd, pipeline_mode=pl.Buffered(1))


def _params(n_axes):
    return pltpu.CompilerParams(dimension_semantics=("arbitrary",) * n_axes, vmem_limit_bytes=VMEM_LIMIT)


def _pool_front_kernel(*refs, prompt, nb, ta, d):
    (x_ref, hist_ref, gmix_ref, gffn_ref, wpool_ref, scale_ref,
     x1_ref, hf_ref, state_ref, e_ref, f2_ref, f4_ref, f8_ref) = refs
    gw = d // len(POOL_WINDOWS)
    j = pl.program_id(1)
    n = ta + HIST
    x = x_ref[...]
    g = gmix_ref[...]
    h = _rms(x, g)
    if prompt:
        hp = _rms(hist_ref[...], g)
        hp = jnp.where(j == 0, 0.0, hp)
        e_ref[:, 0:HIST, :] = hp
    else:
        e_ref[:, 0:1, :] = jnp.zeros((nb, 1, d), F32)
        e_ref[:, 1:HIST, :] = hist_ref[...]
    e_ref[:, HIST:n, :] = h
    zero_tail = jnp.zeros((nb, 8, d), F32)
    e_ref[:, n:n + 8, :] = zero_tail
    f2_ref[:, 0:n, :] = e_ref[:, 0:n, :] + e_ref[:, 1:n + 1, :]
    f2_ref[:, n:n + 8, :] = zero_tail
    f4_ref[:, 0:n, :] = f2_ref[:, 0:n, gw:] + f2_ref[:, 2:n + 2, gw:]
    f4_ref[:, n:n + 8, :] = zero_tail[:, :, gw:]
    f8_ref[:, 0:n, :] = f4_ref[:, 0:n, gw:] + f4_ref[:, 4:n + 4, gw:]
    f8_ref[:, n:n + 8, :] = zero_tail[:, :, 2 * gw:]
    sums = (
        f2_ref[:, HIST - 1:HIST - 1 + ta, 0:gw],
        f4_ref[:, HIST - 3:HIST - 3 + ta, 0:gw],
        f8_ref[:, HIST - 7:HIST - 7 + ta, 0:gw],
        f8_ref[:, HIST - 15:HIST - 15 + ta, gw:] + f8_ref[:, HIST - 7:HIST - 7 + ta, gw:],
    )
    t_idx = lax.broadcasted_iota(jnp.int32, (1, ta, 1), 1) + j * ta
    ys = []
    for gi, w in enumerate(POOL_WINDOWS):
        if prompt:
            cnt = jnp.minimum(t_idx + 1, w).astype(F32)
            mean = sums[gi] / cnt
        else:
            mean = sums[gi] * (1.0 / w)
        diff = mean - h[:, :, gi * gw:(gi + 1) * gw]
        ys.append(_dot(diff.reshape(nb * ta, gw).astype(BF16), wpool_ref[gi]))
    y = jnp.concatenate(ys, axis=-1) * scale_ref[...]
    x1 = x.reshape(nb * ta, d) + y
    x1_ref[...] = x1
    hf_ref[...] = _rms(x1, gffn_ref[...]).astype(BF16)
    if prompt:
        @pl.when(j == pl.num_programs(1) - 1)
        def _():
            state_ref[0] = e_ref[:, n - POOL_BUF:n, :]
    else:
        state_ref[0] = e_ref[:, n - POOL_BUF:n, :]


def _pool_front(x_p, x_s, state, gmix, gffn, wpool, scale, *, ta, nbs):
    bp, sp, d = x_p.shape
    bs, ss, _ = x_s.shape
    tp, ts = bp * sp, bs * ss
    gw = d // len(POOL_WINDOWS)
    assert sp % ta == 0 and ta % HIST == 0 and bs % nbs == 0
    vec = pl.BlockSpec((1, d), lambda *_: (0, 0))
    wspec = pl.BlockSpec((len(POOL_WINDOWS), gw, gw), lambda *_: (0, 0, 0))

    def scratch(nb, rows):
        return [pltpu.VMEM((nb, rows + HIST + 8, d), F32), pltpu.VMEM((nb, rows + HIST + 8, d), F32),
                pltpu.VMEM((nb, rows + HIST + 8, d - gw), F32), pltpu.VMEM((nb, rows + HIST + 8, d - 2 * gw), F32)]

    njp = sp // ta
    x1_p, hf_p, state_p = pl.pallas_call(
        functools.partial(_pool_front_kernel, prompt=True, nb=1, ta=ta, d=d),
        grid=(bp, njp),
        in_specs=[
            pl.BlockSpec((1, ta, d), lambda b, j: (b, j, 0)),
            pl.BlockSpec((1, HIST, d), lambda b, j: (b, jnp.maximum(j * (ta // HIST) - 1, 0), 0)),
            vec, vec, wspec, vec,
        ],
        out_specs=[
            pl.BlockSpec((ta, d), lambda b, j: (b * njp + j, 0)),
            pl.BlockSpec((ta, d), lambda b, j: (b * njp + j, 0)),
            pl.BlockSpec((1, 1, POOL_BUF, d), lambda b, j: (0, b, 0, 0)),
        ],
        out_shape=[jax.ShapeDtypeStruct((tp, d), F32), jax.ShapeDtypeStruct((tp, d), BF16),
                   jax.ShapeDtypeStruct((1, bp, POOL_BUF, d), F32)],
        scratch_shapes=scratch(1, ta),
        compiler_params=_params(2),
        name="pool_front_prompt",
    )(x_p, x_p, gmix, gffn, wpool, scale)

    rows = nbs * ss
    x1_s, hf_s, state_s = pl.pallas_call(
        functools.partial(_pool_front_kernel, prompt=False, nb=nbs, ta=ss, d=d),
        grid=(bs // nbs, 1),
        in_specs=[
            pl.BlockSpec((nbs, ss, d), lambda b, j: (b, 0, 0)),
            pl.BlockSpec((nbs, POOL_BUF, d), lambda b, j: (b, 0, 0)),
            vec, vec, wspec, vec,
        ],
        out_specs=[
            pl.BlockSpec((rows, d), lambda b, j: (b, 0)),
            pl.BlockSpec((rows, d), lambda b, j: (b, 0)),
            pl.BlockSpec((1, nbs, POOL_BUF, d), lambda b, j: (0, b, 0, 0)),
        ],
        out_shape=[jax.ShapeDtypeStruct((ts, d), F32), jax.ShapeDtypeStruct((ts, d), BF16),
                   jax.ShapeDtypeStruct((1, bs, POOL_BUF, d), F32)],
        scratch_shapes=scratch(nbs, ss),
        compiler_params=_params(2),
        name="pool_front_sample",
    )(x_s, state, gmix, gffn, wpool, scale)
    return x1_p, x1_s, hf_p, hf_s, state_p, state_s


def _swiglu_acc(x_bf, w1_ref, w3_ref, w2_ref, acc_ref, fc, before_chunk=None, after_chunk=None):
    d_ff = w1_ref.shape[-1]
    n_chunks = d_ff // fc
    for c in range(n_chunks):
        if before_chunk is not None:
            before_chunk(c, n_chunks)
        sl = slice(c * fc, (c + 1) * fc)
        h1 = _dot(x_bf, w1_ref[:, sl])
        h3 = _dot(x_bf, w3_ref[:, sl])
        a = (h1 * jax.nn.sigmoid(h1)) * h3
        acc_ref[...] += _dot(a.astype(BF16), w2_ref[sl, :])
        if after_chunk is not None:
            after_chunk(c, n_chunks)


def _ple_gate(x, p_bf, gple, wproj_ref, wgate_ref):
    e = _dot(p_bf, wproj_ref[...])
    gate = jax.nn.sigmoid(_dot(_rms(x, gple).astype(BF16), wgate_ref[...]))
    return x + gate * e


def _ffn0_post_kernel(hfp_ref, hfs_ref, x1p_ref, x1s_ref, pp_ref, ps_ref, w1_ref, w3_ref, w2_ref,
                      gple_ref, wproj_ref, wgate_ref, gkv_ref, wk_ref, bk_ref, wv_ref, bv_ref,
                      gmix_ref, wq_ref, bq_ref,
                      x3_ref, k_ref, v_ref, q_ref, acc_ref, *, fc, n_prompt_tiles):
    is_prompt = pl.program_id(0) < n_prompt_tiles
    acc_ref[...] = jnp.where(is_prompt, x1p_ref[...], x1s_ref[...])
    _swiglu_acc(jnp.where(is_prompt, hfp_ref[...], hfs_ref[...]), w1_ref, w3_ref, w2_ref, acc_ref, fc)
    p = jnp.where(is_prompt, pp_ref[...], ps_ref[...]).astype(BF16)
    x3 = _ple_gate(acc_ref[...], p, gple_ref[...], wproj_ref, wgate_ref)
    x3_ref[...] = x3
    kv_in = _rms(x3, gkv_ref[...]).astype(BF16)
    k_ref[...] = _dot(kv_in, wk_ref[...]) + bk_ref[...]
    v_ref[...] = _dot(kv_in, wv_ref[...]) + bv_ref[...]
    hq = _rms(x3, gmix_ref[...]).astype(BF16)
    q_ref[...] = (_dot(hq, wq_ref[...]) + bq_ref[...]).astype(BF16)


def _split_specs(tm, width, n_prompt_tiles, layer=None):
    if layer is None:
        return [pl.BlockSpec((tm, width), lambda i, *_: (jnp.minimum(i, n_prompt_tiles - 1), 0)),
                pl.BlockSpec((tm, width), lambda i, *_: (jnp.maximum(i - n_prompt_tiles, 0), 0))]
    return [pl.BlockSpec((None, tm, width), lambda i, *_: (layer, jnp.minimum(i, n_prompt_tiles - 1), 0)),
            pl.BlockSpec((None, tm, width), lambda i, *_: (layer, jnp.maximum(i - n_prompt_tiles, 0), 0))]


def _ffn0_post(hf_p, hf_s, x1_p, x1_s, pp, ps, w1, w3, w2, gple, wproj, wgate, gkv, wk, bk, wv, bv, gmix, wq, bq,
               *, tm, fc):
    tp, d = x1_p.shape
    ts = x1_s.shape[0]
    t = tp + ts
    kvw = wk.shape[1]
    assert ts % tm == 0 and tp % tm == 0 and w1.shape[1] % fc == 0
    npt = tp // tm
    row = lambda w: pl.BlockSpec((tm, w), lambda i: (i, 0))
    vec = lambda w: pl.BlockSpec((1, w), lambda i: (0, 0))
    return pl.pallas_call(
        functools.partial(_ffn0_post_kernel, fc=fc, n_prompt_tiles=npt),
        grid=(t // tm,),
        in_specs=_split_specs(tm, d, npt) + _split_specs(tm, d, npt) + _split_specs(tm, pp.shape[2], npt, layer=0) + [
            _const_spec(w1.shape), _const_spec(w3.shape), _const_spec(w2.shape),
            vec(d), _const_spec(wproj.shape), _const_spec(wgate.shape),
            vec(d), _const_spec(wk.shape), vec(kvw), _const_spec(wv.shape), vec(kvw),
            vec(d), _const_spec(wq.shape), vec(d),
        ],
        out_specs=[row(d), row(kvw), row(kvw), row(d)],
        out_shape=[jax.ShapeDtypeStruct((t, d), F32), jax.ShapeDtypeStruct((t, kvw), F32),
                   jax.ShapeDtypeStruct((t, kvw), F32), jax.ShapeDtypeStruct((t, d), BF16)],
        scratch_shapes=[pltpu.VMEM((tm, d), F32)],
        compiler_params=_params(1),
        name="ffn0_post",
    )(hf_p, hf_s, x1_p, x1_s, pp, ps, w1, w3, w2, gple, wproj, wgate, gkv, wk, bk, wv, bv, gmix, wq, bq)


def _swa_kernel(*refs, prompt, tq, n_kv, group, hd):
    if prompt:
        sink_ref, q_ref, ka_ref, kb_ref, va_ref, vb_ref, o_ref = refs
        kk = jnp.concatenate([ka_ref[...], kb_ref[...]], axis=0)
        vv = jnp.concatenate([va_ref[...], vb_ref[...]], axis=0)
        first_chunk = N_BACK - pl.program_id(1) * (tq // CHUNK)
    else:
        sink_ref, q_ref, ck_ref, kn_ref, cv_ref, vn_ref, o_ref = refs
        pad = jnp.zeros(kn_ref.shape, F32)
        kk = jnp.concatenate([ck_ref[0], kn_ref[...], pad], axis=0)
        vv = jnp.concatenate([cv_ref[0], vn_ref[...], pad], axis=0)
        first_chunk = 0
    nk = kk.shape[0]
    lanes = 128
    per = lanes // hd
    assert per == 2 and group % per == 0 and nk % lanes == 0
    qc = lax.broadcasted_iota(jnp.int32, (tq, per * nk), 0) // CHUNK
    kc = (lax.broadcasted_iota(jnp.int32, (tq, per * nk), 1) % nk) // CHUNK
    bias = jnp.where(kc >= jnp.maximum(qc, first_chunk), jnp.where(kc <= qc + N_BACK, 0.0, NEG), NEG)
    lane = lax.broadcasted_iota(jnp.int32, (nk, lanes), 1)
    ones_a = jnp.where(lane < hd, 1.0, 0.0)
    ones_b = jnp.where(lane >= hd, 1.0, 0.0)
    out_lane = lax.broadcasted_iota(jnp.int32, (tq, lanes), 1)
    for kvh in range(n_kv):
        tile, half = divmod(kvh, per)
        keep = (lane < hd) if half == 0 else (lane >= hd)

        def both_halves(x):
            own = jnp.where(keep, x[:, tile * lanes:(tile + 1) * lanes], 0.0)
            other = pltpu.roll(own, hd, axis=1)
            return (own, other) if half == 0 else (other, own)

        k_a, k_b = both_halves(kk * (hd ** -0.5))
        v_a, v_b = both_halves(vv)
        kbd = jnp.concatenate([k_a, k_b], axis=0).astype(BF16)
        vobd = jnp.concatenate([jnp.concatenate([v_a, ones_a], axis=1),
                                jnp.concatenate([v_b, ones_b], axis=1)], axis=0).astype(BF16)
        for pr in range(group // per):
            head_a = kvh * group + pr * per
            col0 = head_a * hd
            s = lax.dot_general(q_ref[:, col0:col0 + lanes], kbd, (((1,), (1,)), ((), ())),
                                preferred_element_type=F32) + bias
            sink_a, sink_b = sink_ref[head_a], sink_ref[head_a + 1]
            m_a = jnp.maximum(jnp.max(s[:, :nk], axis=-1, keepdims=True), sink_a)
            m_b = jnp.maximum(jnp.max(s[:, nk:], axis=-1, keepdims=True), sink_b)
            p = jnp.concatenate([jnp.exp(s[:, :nk] - m_a), jnp.exp(s[:, nk:] - m_b)], axis=1).astype(BF16)
            r = _dot(p, vobd)
            sink_term = jnp.where(out_lane < hd, jnp.exp(sink_a - m_a), jnp.exp(sink_b - m_b))
            o_ref[:, col0:col0 + lanes] = (r[:, :lanes] / (r[:, lanes:] + sink_term)).astype(BF16)


def _swa(q, k, v, cache_k, cache_v, sinks, *, bp, sp, bs, ss, tq):
    t, d = q.shape
    kvw = k.shape[1]
    tp = bp * sp
    n_heads = sinks.shape[0]
    hd = d // n_heads
    n_kv = kvw // hd
    back = N_BACK * CHUNK
    assert tq == back and sp % tq == 0 and ss == CHUNK and cache_k.shape[1] == back
    smem = pl.BlockSpec(memory_space=pltpu.SMEM)
    nj = sp // tq
    kw = dict(n_kv=n_kv, group=n_heads // n_kv, hd=hd)
    prev_blk = pl.BlockSpec((tq, kvw), lambda b, j: (b * nj + jnp.maximum(j - 1, 0), 0))
    this_blk = pl.BlockSpec((tq, kvw), lambda b, j: (b * nj + j, 0))
    o_p = pl.pallas_call(
        functools.partial(_swa_kernel, prompt=True, tq=tq, **kw),
        grid=(bp, nj),
        in_specs=[smem, pl.BlockSpec((tq, d), lambda b, j: (b * nj + j, 0)), prev_blk, this_blk, prev_blk, this_blk],
        out_specs=pl.BlockSpec((tq, d), lambda b, j: (b * nj + j, 0)),
        out_shape=jax.ShapeDtypeStruct((tp, d), BF16),
        compiler_params=_params(2),
        name="swa_prompt",
    )(sinks, q, k, k, v, v)
    off = tp // ss
    cache_blk = pl.BlockSpec((1, back, kvw), lambda b: (b, 0, 0))
    new_blk = pl.BlockSpec((ss, kvw), lambda b: (off + b, 0))
    o_s = pl.pallas_call(
        functools.partial(_swa_kernel, prompt=False, tq=ss, **kw),
        grid=(bs,),
        in_specs=[smem, pl.BlockSpec((ss, d), lambda b: (off + b, 0)), cache_blk, new_blk, cache_blk, new_blk],
        out_specs=pl.BlockSpec((ss, d), lambda b: (b, 0)),
        out_shape=jax.ShapeDtypeStruct((bs * ss, d), BF16),
        compiler_params=_params(1),
        name="swa_sample",
    )(sinks, q, cache_k.reshape(bs, back, kvw), k, cache_v.reshape(bs, back, kvw), v)
    return o_p, o_s


def _oproj_route_kernel(op_ref, os_ref, x3_ref, wo_ref, gffn_ref, wr_ref, br_ref, x4_ref, hm_ref, route_ref,
                        *, n_prompt_tiles):
    o = jnp.where(pl.program_id(0) < n_prompt_tiles, op_ref[...], os_ref[...])
    x4 = x3_ref[...] + _dot(o, wo_ref[...])
    x4_ref[...] = x4
    hm = _rms(x4, gffn_ref[...])
    hm_ref[...] = hm
    hm_hi = hm.astype(BF16)
    hm_lo = (hm - hm_hi.astype(F32)).astype(BF16)
    wr = wr_ref[...]
    wr_hi = wr.astype(BF16)
    wr_lo = (wr - wr_hi.astype(F32)).astype(BF16)
    logits = (_dot(hm_hi, wr_hi) + (_dot(hm_lo, wr_hi) + _dot(hm_hi, wr_lo))) + br_ref[...]
    lane = lax.broadcasted_iota(jnp.int32, logits.shape, 1).astype(F32)
    m1 = jnp.max(logits, axis=-1, keepdims=True)
    i1 = jnp.min(jnp.where(logits == m1, lane, float(ROUTE_LANES)), axis=-1, keepdims=True)
    rest = jnp.where(lane == i1, -jnp.inf, logits)
    m2 = jnp.max(rest, axis=-1, keepdims=True)
    i2 = jnp.min(jnp.where(rest == m2, lane, float(ROUTE_LANES)), axis=-1, keepdims=True)
    e2 = jnp.exp(m2 - m1)
    g1 = 1.0 / (1.0 + e2)
    g2 = e2 / (1.0 + e2)
    route = jnp.where(lane == 0.0, i1, jnp.where(lane == 1.0, i2,
                      jnp.where(lane == 2.0, g1, jnp.where(lane == 3.0, g2, 0.0))))
    route_ref[...] = route


def _oproj_route(o_p, o_s, x3, wo, gffn, wr, br, *, tm):
    t, d = x3.shape
    tp = o_p.shape[0]
    assert tp % tm == 0 and o_s.shape[0] % tm == 0
    row = lambda w: pl.BlockSpec((tm, w), lambda i: (i, 0))
    return pl.pallas_call(
        functools.partial(_oproj_route_kernel, n_prompt_tiles=tp // tm),
        grid=(t // tm,),
        in_specs=_split_specs(tm, d, tp // tm) + [row(d), _const_spec(wo.shape), pl.BlockSpec((1, d), lambda i: (0, 0)),
                  _const_spec(wr.shape), pl.BlockSpec((1, ROUTE_LANES), lambda i: (0, 0))],
        out_specs=[row(d), row(d), row(ROUTE_LANES)],
        out_shape=[jax.ShapeDtypeStruct((t, d), F32), jax.ShapeDtypeStruct((t, d), F32),
                   jax.ShapeDtypeStruct((t, ROUTE_LANES), F32)],
        compiler_params=_params(1),
        name="oproj_route",
    )(o_p, o_s, x3, wo, gffn, wr, br)


def _moe_kernel(texp_ref, first_ref, rtok_ref, rdst_ref, hm_ref, w1_hbm, w3_hbm, w2_hbm, y_ref,
                xbuf, obuf, acc_ref, w1_ref, w3_ref, w2_ref, st13, st2, gsem, ssem, wsem,
                *, tm, fc, n_steps):
    i = pl.program_id(0)
    n_slots = st2.shape[0]

    def gather_row(base, r, xb, sem):
        tok = rtok_ref[base + r]
        pltpu.make_async_copy(hm_ref.at[pl.ds(tok, 1), :], xb.at[pl.ds(r, 1), :], sem).start()

    def scatter_row(base, r, ob, sem):
        dst = rdst_ref[base + r]
        pltpu.make_async_copy(ob.at[pl.ds(r, 1), :], y_ref.at[pl.ds(dst, 1), :], sem).start()

    def weight_copies(c):
        e = texp_ref[i]
        slot = c % n_slots
        cols = pl.ds(c * fc, fc)
        return (pltpu.make_async_copy(w1_hbm.at[e, :, cols], st13.at[slot, 0], wsem.at[slot]),
                pltpu.make_async_copy(w3_hbm.at[e, :, cols], st13.at[slot, 1], wsem.at[slot]),
                pltpu.make_async_copy(w2_hbm.at[e, cols, :], st2.at[slot], wsem.at[slot]))

    def start_weights(c):
        for cp in weight_copies(c):
            cp.start(1)

    def stage_weights(c, n_chunks):
        for cp in weight_copies(c):
            cp.wait()
        slot = c % n_slots
        w1_ref[:, c * fc:(c + 1) * fc] = st13[slot, 0].astype(BF16)
        w3_ref[:, c * fc:(c + 1) * fc] = st13[slot, 1].astype(BF16)
        w2_ref[c * fc:(c + 1) * fc, :] = st2[slot].astype(BF16)
        if c + n_slots < n_chunks:
            start_weights(c + n_slots)

    def wait_gather(xb, sem):
        pltpu.make_async_copy(hm_ref.at[pl.ds(0, tm), :], xb, sem).wait()

    def wait_scatter(ob, sem):
        pltpu.make_async_copy(ob, y_ref.at[pl.ds(0, tm), :], sem).wait()

    @pl.when(i == 0)
    def _():
        obuf[...] = jnp.zeros(obuf.shape, F32)
        lax.fori_loop(0, tm, lambda r, c: (gather_row(0, r, xbuf.at[0], gsem.at[0]), c)[1], 0)

    is_first = first_ref[i] == 1

    @pl.when(is_first)
    def _():
        for c in range(n_slots):
            start_weights(c)

    p = i % 2
    xb, ob, xb_next, ob_prev = xbuf.at[p], obuf.at[p], xbuf.at[1 - p], obuf.at[1 - p]

    def step(streaming):
        wait_gather(xb, gsem.at[p])
        acc_ref[...] = jnp.zeros(acc_ref.shape, F32)
        g_base = (i + 1) * tm
        s_base = i * tm

        def move_rows(c, n_chunks):
            per = -(-2 * tm // n_chunks)
            for m in range(c * per, min((c + 1) * per, 2 * tm)):
                if m < tm:
                    gather_row(g_base, m, xb_next, gsem.at[1 - p])
                else:
                    scatter_row(s_base, m - tm, ob_prev, ssem.at[1 - p])

        _swiglu_acc(xb[...].astype(BF16), w1_ref, w3_ref, w2_ref, acc_ref, fc,
                    before_chunk=stage_weights if streaming else None, after_chunk=move_rows)

        @pl.when(i >= 1)
        def _():
            wait_scatter(ob, ssem.at[p])
        ob[...] = acc_ref[...]

    @pl.when(is_first)
    def _():
        step(True)

    @pl.when(jnp.logical_not(is_first))
    def _():
        step(False)

    @pl.when(i == n_steps - 1)
    def _():
        wait_gather(xb_next, gsem.at[1 - p])
        wait_scatter(ob_prev, ssem.at[1 - p])


def _moe(hm, route, w1, w3, w2, *, tm, fc):
    t, d = hm.shape
    n_exp = w1.shape[0]
    n_assign = t * TOP_K
    nt = n_assign // tm + n_exp
    assert n_assign % tm == 0
    e_flat = route[:, :TOP_K].astype(jnp.int32).reshape(-1)
    order = jnp.argsort(e_flat, stable=True).astype(jnp.int32)
    counts = jnp.sum((e_flat[:, None] == jnp.arange(n_exp, dtype=jnp.int32)[None, :]).astype(jnp.int32), axis=0)
    tiles_e = (counts + tm - 1) // tm
    tile_end = jnp.cumsum(tiles_e)
    tile_start = tile_end - tiles_e
    cnt_start = jnp.cumsum(counts) - counts
    tile_ids = jnp.arange(nt, dtype=jnp.int32)
    texp = jnp.minimum(jnp.sum((tile_ids[:, None] >= tile_end[None, :]).astype(jnp.int32), axis=1), n_exp - 1)
    within = (tile_ids - tile_start[texp]) * tm
    nval = jnp.where(tile_ids < tile_end[-1], jnp.clip(counts[texp] - within, 0, tm), 0).astype(jnp.int32)
    r_in = jnp.arange(tm, dtype=jnp.int32)[None, :]
    src = jnp.clip(cnt_start[texp][:, None] + within[:, None] + r_in, 0, n_assign - 1)
    real = r_in < nval[:, None]
    assign = jnp.where(real, order[src], 0)
    scratch_row = TOP_K * t + (tile_ids[:, None] % 2) * tm + r_in
    rtok = jnp.concatenate([(assign // TOP_K).reshape(-1), jnp.zeros((2 * tm,), jnp.int32)])
    rdst = jnp.concatenate([TOP_K * t + tm + r_in[0],
                            jnp.where(real, (assign % TOP_K) * t + assign // TOP_K, scratch_row).reshape(-1)])
    texp = jnp.concatenate([texp, texp[-1:]])
    prev_exp = jnp.concatenate([jnp.full((1,), -1, jnp.int32), texp[:-1]])
    first = jnp.where(jnp.arange(nt + 1) < tile_end[-1], texp != prev_exp, False).astype(jnp.int32)
    n_steps = nt + 1
    d_ff = w1.shape[2]
    n_slots = 2
    assert d_ff % fc == 0 and d_ff // fc >= n_slots

    hbm = pl.BlockSpec(memory_space=pl.ANY)
    y2 = pl.pallas_call(
        functools.partial(_moe_kernel, tm=tm, fc=fc, n_steps=n_steps),
        grid_spec=pltpu.PrefetchScalarGridSpec(
            num_scalar_prefetch=4,
            grid=(n_steps,),
            in_specs=[hbm, hbm, hbm, hbm],
            out_specs=hbm,
            scratch_shapes=[pltpu.VMEM((2, tm, d), F32), pltpu.VMEM((2, tm, d), F32), pltpu.VMEM((tm, d), F32),
                            pltpu.VMEM((d, d_ff), BF16), pltpu.VMEM((d, d_ff), BF16), pltpu.VMEM((d_ff, d), BF16),
               pltpu.VMEM((n_slots, 2, d, fc), F32), pltpu.VMEM((n_slots, fc, d), F32),
               pltpu.SemaphoreType.DMA((2,)), pltpu.SemaphoreType.DMA((2,)), pltpu.SemaphoreType.DMA((n_slots,))],
        ),
        out_shape=jax.ShapeDtypeStruct((TOP_K * t + 2 * tm, d), F32),
        compiler_params=_params(1),
        name="moe",
    )(texp, first, rtok, rdst, hm, w1, w3, w2)
    return y2


def _final_kernel(x4_ref, ya_ref, yb_ref, route_ref, pp_ref, ps_ref, gple_ref, wproj_ref, wgate_ref, gfin_ref,
                  yp_ref, ys_ref, *, n_prompt_tiles):
    i = pl.program_id(0)
    route = route_ref[...]
    g1 = route[:, 2:3]
    g2 = route[:, 3:4]
    x5 = x4_ref[...] + (g1 * ya_ref[...] + g2 * yb_ref[...])
    p = jnp.where(i < n_prompt_tiles, pp_ref[...], ps_ref[...]).astype(BF16)
    x6 = _ple_gate(x5, p, gple_ref[...], wproj_ref, wgate_ref)
    y = _rms(x6, gfin_ref[...])

    @pl.when(i < n_prompt_tiles)
    def _():
        yp_ref[...] = y

    @pl.when(i >= n_prompt_tiles)
    def _():
        ys_ref[...] = y


def _final(x4, y2, route, pp, ps, gple, wproj, wgate, gfin, *, tm):
    t, d = x4.shape
    tp, ts = pp.shape[1], ps.shape[1]
    npt = tp // tm
    assert tp % tm == 0 and ts % tm == 0
    row = lambda w: pl.BlockSpec((tm, w), lambda i: (i, 0))
    vec = pl.BlockSpec((1, d), lambda i: (0, 0))
    return pl.pallas_call(
        functools.partial(_final_kernel, n_prompt_tiles=npt),
        grid=(t // tm,),
        in_specs=[row(d), row(d), pl.BlockSpec((tm, d), lambda i: (t // tm + i, 0)), row(ROUTE_LANES)]
        + _split_specs(tm, pp.shape[2], npt, layer=1) + [vec, _const_spec(wproj.shape), _const_spec(wgate.shape), vec],
        out_specs=_split_specs(tm, d, npt),
        out_shape=[jax.ShapeDtypeStruct((tp, d), F32), jax.ShapeDtypeStruct((ts, d), F32)],
        compiler_params=_params(1),
        name="final",
    )(x4, y2, y2, route, pp, ps, gple, wproj, wgate, gfin)


def kernel(x_prompt, x_sample, state_pool, cache_k, cache_v, p_prompt, p_sample, g_mix, g_ffn, g_ple, g_kv, g_final,
           w_pool, pool_scale, w_q, b_q, w_k, b_k, w_v, b_v, w_o, sinks, w1_dense, w3_dense, w2_dense, w_router,
           b_router, w1_moe, w3_moe, w2_moe, w_ple_proj, w_ple_gate):
    bp, sp, d = x_prompt.shape
    bs, ss, _ = x_sample.shape
    depth = g_mix.shape[0]
    assert depth == 2 and w_pool.shape[0] == 1 and w_q.shape[0] == 1 and w1_dense.shape[0] == 1 and w1_moe.shape[0] == 1
    tp, ts = bp * sp, bs * ss
    n_kv, hd = cache_k.shape[2], cache_k.shape[3]
    n_exp = w_router.shape[2]
    ple = p_prompt.shape[-1]
    bf = lambda w: w.astype(BF16)
    vec = lambda g: g.reshape(1, -1)
    pp = p_prompt.reshape(depth, tp, ple)
    ps = p_sample.reshape(depth, ts, ple)

    x1_p, x1_s, hf_p, hf_s, pool_p, pool_s = _pool_front(
        x_prompt, x_sample, state_pool[0], vec(g_mix[0]), vec(g_ffn[0]), bf(w_pool[0]), vec(pool_scale[0]),
        ta=512, nbs=8)

    x3, k, v, q = _ffn0_post(
        hf_p, hf_s, x1_p, x1_s, pp, ps,
        bf(w1_dense[0]), bf(w3_dense[0]), bf(w2_dense[0]),
        vec(g_ple[0]), bf(w_ple_proj[0]), bf(w_ple_gate[0]),
        vec(g_kv), bf(w_k), vec(b_k), bf(w_v), vec(b_v),
        vec(g_mix[1]), bf(w_q[0]), vec(b_q[0]), tm=512, fc=512)

    o_p, o_s = _swa(q, k, v, cache_k, cache_v, sinks[0], bp=bp, sp=sp, bs=bs, ss=ss, tq=N_BACK * CHUNK)

    wr = jnp.pad(w_router[0], ((0, 0), (0, ROUTE_LANES - n_exp)))
    br = jnp.pad(b_router[0], (0, ROUTE_LANES - n_exp), constant_values=NEG).reshape(1, ROUTE_LANES)
    x4, hm, route = _oproj_route(o_p, o_s, x3, bf(w_o[0]), vec(g_ffn[1]), wr, br, tm=512)

    y2 = _moe(hm, route, w1_moe[0], w3_moe[0], w2_moe[0], tm=512, fc=512)

    y_p, y_s = _final(x4, y2, route, pp, ps,
                      vec(g_ple[1]), bf(w_ple_proj[1]), bf(w_ple_gate[1]), vec(g_final), tm=512)

    back = N_BACK * CHUNK

    def new_cache(rows, cache):
        b = rows.shape[0] // (sp if cache is None else ss)
        rows = rows.reshape(b, -1, n_kv, hd)
        if cache is not None:
            rows = jnp.concatenate([cache, rows], axis=1)
        return rows[:, -back:]

    return (y_p.reshape(bp, sp, d), y_s.reshape(bs, ss, d), pool_p, pool_s,
            new_cache(k[:tp], None), new_cache(v[:tp], None), new_cache(k[tp:], cache_k), new_cache(v[tp:], cache_v))
```

```python
import functools

import jax
import jax.numpy as jnp
from jax import lax
from jax.experimental import pallas as pl
from jax.experimental.pallas import tpu as pltpu

EPS = 1e-6
NEG = -1e30
CHUNK = 64
N_BACK = 2
POOL_WINDOWS = (2, 4, 8, 16)
POOL_BUF = 15
HIST = 16
TOP_K = 2
ROUTE_LANES = 128
VMEM_LIMIT = 56 * 1024 * 1024

F32 = jnp.float32
BF16 = jnp.bfloat16


def _rms(x, g):
    return (x * lax.rsqrt(jnp.mean(x * x, axis=-1, keepdims=True) + EPS)) * g


def _dot(a, b):
    return jnp.dot(a, b, preferred_element_type=F32)


def _const_spec(shape):
    nd = len(shape)
    return pl.BlockSpec(shape, lambda *_: (0,) * nd, pipeline_mode=pl.Buffered(1))


def _params(n_axes):
    return pltpu.CompilerParams(dimension_semantics=("arbitrary",) * n_axes, vmem_limit_bytes=VMEM_LIMIT)


def _pool_front_kernel(*refs, prompt, nb, ta, d):
    (x_ref, hist_ref, gmix_ref, gffn_ref, wpool_ref, scale_ref,
     x1_ref, hf_ref, state_ref, e_ref, f2_ref, f4_ref, f8_ref) = refs
    gw = d // len(POOL_WINDOWS)
    j = pl.program_id(1)
    n = ta + HIST
    x = x_ref[...]
    g = gmix_ref[...]
    h = _rms(x, g)
    if prompt:
        hp = _rms(hist_ref[...], g)
        hp = jnp.where(j == 0, 0.0, hp)
        e_ref[:, 0:HIST, :] = hp
    else:
        e_ref[:, 0:1, :] = jnp.zeros((nb, 1, d), F32)
        e_ref[:, 1:HIST, :] = hist_ref[...]
    e_ref[:, HIST:n, :] = h
    zero_tail = jnp.zeros((nb, 8, d), F32)
    e_ref[:, n:n + 8, :] = zero_tail
    f2_ref[:, 0:n, :] = e_ref[:, 0:n, :] + e_ref[:, 1:n + 1, :]
    f2_ref[:, n:n + 8, :] = zero_tail
    f4_ref[:, 0:n, :] = f2_ref[:, 0:n, gw:] + f2_ref[:, 2:n + 2, gw:]
    f4_ref[:, n:n + 8, :] = zero_tail[:, :, gw:]
    f8_ref[:, 0:n, :] = f4_ref[:, 0:n, gw:] + f4_ref[:, 4:n + 4, gw:]
    f8_ref[:, n:n + 8, :] = zero_tail[:, :, 2 * gw:]
    sums = (
        f2_ref[:, HIST - 1:HIST - 1 + ta, 0:gw],
        f4_ref[:, HIST - 3:HIST - 3 + ta, 0:gw],
        f8_ref[:, HIST - 7:HIST - 7 + ta, 0:gw],
        f8_ref[:, HIST - 15:HIST - 15 + ta, gw:] + f8_ref[:, HIST - 7:HIST - 7 + ta, gw:],
    )
    t_idx = lax.broadcasted_iota(jnp.int32, (1, ta, 1), 1) + j * ta
    ys = []
    for gi, w in enumerate(POOL_WINDOWS):
        if prompt:
            cnt = jnp.minimum(t_idx + 1, w).astype(F32)
            mean = sums[gi] / cnt
        else:
            mean = sums[gi] * (1.0 / w)
        diff = mean - h[:, :, gi * gw:(gi + 1) * gw]
        ys.append(_dot(diff.reshape(nb * ta, gw).astype(BF16), wpool_ref[gi]))
    y = jnp.concatenate(ys, axis=-1) * scale_ref[...]
    x1 = x.reshape(nb * ta, d) + y
    x1_ref[...] = x1
    hf_ref[...] = _rms(x1, gffn_ref[...]).astype(BF16)
    if prompt:
        @pl.when(j == pl.num_programs(1) - 1)
        def _():
            state_ref[0] = e_ref[:, n - POOL_BUF:n, :]
    else:
        state_ref[0] = e_ref[:, n - POOL_BUF:n, :]


def _pool_front(x_p, x_s, state, gmix, gffn, wpool, scale, *, ta, nbs):
    bp, sp, d = x_p.shape
    bs, ss, _ = x_s.shape
    tp, ts = bp * sp, bs * ss
    gw = d // len(POOL_WINDOWS)
    assert sp % ta == 0 and ta % HIST == 0 and bs % nbs == 0
    vec = pl.BlockSpec((1, d), lambda *_: (0, 0))
    wspec = pl.BlockSpec((len(POOL_WINDOWS), gw, gw), lambda *_: (0, 0, 0))

    def scratch(nb, rows):
        return [pltpu.VMEM((nb, rows + HIST + 8, d), F32), pltpu.VMEM((nb, rows + HIST + 8, d), F32),
                pltpu.VMEM((nb, rows + HIST + 8, d - gw), F32), pltpu.VMEM((nb, rows + HIST + 8, d - 2 * gw), F32)]

    njp = sp // ta
    x1_p, hf_p, state_p = pl.pallas_call(
        functools.partial(_pool_front_kernel, prompt=True, nb=1, ta=ta, d=d),
        grid=(bp, njp),
        in_specs=[
            pl.BlockSpec((1, ta, d), lambda b, j: (b, j, 0)),
            pl.BlockSpec((1, HIST, d), lambda b, j: (b, jnp.maximum(j * (ta // HIST) - 1, 0), 0)),
            vec, vec, wspec, vec,
        ],
        out_specs=[
            pl.BlockSpec((ta, d), lambda b, j: (b * njp + j, 0)),
            pl.BlockSpec((ta, d), lambda b, j: (b * njp + j, 0)),
            pl.BlockSpec((1, 1, POOL_BUF, d), lambda b, j: (0, b, 0, 0)),
        ],
        out_shape=[jax.ShapeDtypeStruct((tp, d), F32), jax.ShapeDtypeStruct((tp, d), BF16),
                   jax.ShapeDtypeStruct((1, bp, POOL_BUF, d), F32)],
        scratch_shapes=scratch(1, ta),
        compiler_params=_params(2),
        name="pool_front_prompt",
    )(x_p, x_p, gmix, gffn, wpool, scale)

    rows = nbs * ss
    x1_s, hf_s, state_s = pl.pallas_call(
        functools.partial(_pool_front_kernel, prompt=False, nb=nbs, ta=ss, d=d),
        grid=(bs // nbs, 1),
        in_specs=[
            pl.BlockSpec((nbs, ss, d), lambda b, j: (b, 0, 0)),
            pl.BlockSpec((nbs, POOL_BUF, d), lambda b, j: (b, 0, 0)),
            vec, vec, wspec, vec,
        ],
        out_specs=[
            pl.BlockSpec((rows, d), lambda b, j: (b, 0)),
            pl.BlockSpec((rows, d), lambda b, j: (b, 0)),
            pl.BlockSpec((1, nbs, POOL_BUF, d), lambda b, j: (0, b, 0, 0)),
        ],
        out_shape=[jax.ShapeDtypeStruct((ts, d), F32), jax.ShapeDtypeStruct((ts, d), BF16),
                   jax.ShapeDtypeStruct((1, bs, POOL_BUF, d), F32)],
        scratch_shapes=scratch(nbs, ss),
        compiler_params=_params(2),
        name="pool_front_sample",
    )(x_s, state, gmix, gffn, wpool, scale)
    return x1_p, x1_s, hf_p, hf_s, state_p, state_s


def _swiglu_acc(x_bf, w1_ref, w3_ref, w2_ref, acc_ref, fc, before_chunk=None, after_chunk=None):
    d_ff = w1_ref.shape[-1]
    n_chunks = d_ff // fc
    for c in range(n_chunks):
        if before_chunk is not None:
            before_chunk(c, n_chunks)
        sl = slice(c * fc, (c + 1) * fc)
        h1 = _dot(x_bf, w1_ref[:, sl])
        h3 = _dot(x_bf, w3_ref[:, sl])
        a = (h1 * jax.nn.sigmoid(h1)) * h3
        acc_ref[...] += _dot(a.astype(BF16), w2_ref[sl, :])
        if after_chunk is not None:
            after_chunk(c, n_chunks)


def _ple_gate(x, p_bf, gple, wproj_ref, wgate_ref):
    e = _dot(p_bf, wproj_ref[...])
    gate = jax.nn.sigmoid(_dot(_rms(x, gple).astype(BF16), wgate_ref[...]))
    return x + gate * e


def _ffn0_post_kernel(hfp_ref, hfs_ref, x1p_ref, x1s_ref, pp_ref, ps_ref, w1_ref, w3_ref, w2_ref,
                      gple_ref, wproj_ref, wgate_ref, gkv_ref, wk_ref, bk_ref, wv_ref, bv_ref,
                      gmix_ref, wq_ref, bq_ref,
                      x3_ref, k_ref, v_ref, q_ref, kcp_ref, vcp_ref, acc_ref,
                      *, fc, n_prompt_tiles, tiles_per_seq):
    i = pl.program_id(0)
    is_prompt = i < n_prompt_tiles
    acc_ref[...] = jnp.where(is_prompt, x1p_ref[...], x1s_ref[...])
    _swiglu_acc(jnp.where(is_prompt, hfp_ref[...], hfs_ref[...]), w1_ref, w3_ref, w2_ref, acc_ref, fc)
    p = jnp.where(is_prompt, pp_ref[...], ps_ref[...]).astype(BF16)
    x3 = _ple_gate(acc_ref[...], p, gple_ref[...], wproj_ref, wgate_ref)
    x3_ref[...] = x3
    kv_in = _rms(x3, gkv_ref[...]).astype(BF16)
    k = _dot(kv_in, wk_ref[...]) + bk_ref[...]
    v = _dot(kv_in, wv_ref[...]) + bv_ref[...]
    k_ref[...] = k
    v_ref[...] = v
    hq = _rms(x3, gmix_ref[...]).astype(BF16)
    q_ref[...] = (_dot(hq, wq_ref[...]) + bq_ref[...]).astype(BF16)

    back = kcp_ref.shape[1]

    @pl.when(jnp.logical_and(is_prompt, i % tiles_per_seq == tiles_per_seq - 1))
    def _():
        kcp_ref[0] = k[k.shape[0] - back:]
        vcp_ref[0] = v[v.shape[0] - back:]


def _split_specs(tm, width, n_prompt_tiles, layer=None, lean=False):
    kw = dict(pipeline_mode=pl.Buffered(1)) if lean else {}
    if layer is None:
        return [pl.BlockSpec((tm, width), lambda i, *_: (jnp.minimum(i, n_prompt_tiles - 1), 0)),
                pl.BlockSpec((tm, width), lambda i, *_: (jnp.maximum(i - n_prompt_tiles, 0), 0), **kw)]
    return [pl.BlockSpec((None, tm, width), lambda i, *_: (layer, jnp.minimum(i, n_prompt_tiles - 1), 0)),
            pl.BlockSpec((None, tm, width), lambda i, *_: (layer, jnp.maximum(i - n_prompt_tiles, 0), 0), **kw)]


def _ffn0_post(hf_p, hf_s, x1_p, x1_s, pp, ps, w1, w3, w2, gple, wproj, wgate, gkv, wk, bk, wv, bv,
               gmix, wq, bq, *, tm, fc, sp, back):
    tp, d = x1_p.shape
    ts = x1_s.shape[0]
    t = tp + ts
    kvw = wk.shape[1]
    bp = tp // sp
    assert ts % tm == 0 and tp % tm == 0 and w1.shape[1] % fc == 0 and sp % tm == 0 and back <= tm
    npt = tp // tm
    row = lambda w: pl.BlockSpec((tm, w), lambda i: (i, 0))
    vec = lambda w: pl.BlockSpec((1, w), lambda i: (0, 0))
    pcache = pl.BlockSpec((1, back, kvw), lambda i: (jnp.minimum(i // (sp // tm), bp - 1), 0, 0))
    cache_shape = jax.ShapeDtypeStruct((bp, back, kvw), F32)
    return pl.pallas_call(
        functools.partial(_ffn0_post_kernel, fc=fc, n_prompt_tiles=npt, tiles_per_seq=sp // tm),
        grid=(t // tm,),
        in_specs=_split_specs(tm, d, npt) + _split_specs(tm, d, npt) + _split_specs(tm, pp.shape[2], npt, layer=0) + [
            _const_spec(w1.shape), _const_spec(w3.shape), _const_spec(w2.shape),
            vec(d), _const_spec(wproj.shape), _const_spec(wgate.shape),
            vec(d), _const_spec(wk.shape), vec(kvw), _const_spec(wv.shape), vec(kvw),
            vec(d), _const_spec(wq.shape), vec(d),
        ],
        out_specs=[row(d), row(kvw), row(kvw), row(d), pcache, pcache],
        out_shape=[jax.ShapeDtypeStruct((t, d), F32), jax.ShapeDtypeStruct((t, kvw), F32),
                   jax.ShapeDtypeStruct((t, kvw), F32), jax.ShapeDtypeStruct((t, d), BF16), cache_shape, cache_shape],
        scratch_shapes=[pltpu.VMEM((tm, d), F32)],
        compiler_params=_params(1),
        name="ffn0_post",
    )(hf_p, hf_s, x1_p, x1_s, pp, ps, w1, w3, w2, gple, wproj, wgate, gkv, wk, bk, wv, bv, gmix, wq, bq)


def _swa_kernel(*refs, prompt, tq, n_kv, group, hd):
    if prompt:
        sink_ref, q_ref, ka_ref, kb_ref, va_ref, vb_ref, o_ref = refs
        kk = jnp.concatenate([ka_ref[...], kb_ref[...]], axis=0)
        vv = jnp.concatenate([va_ref[...], vb_ref[...]], axis=0)
        first_chunk = N_BACK - pl.program_id(1) * (tq // CHUNK)
    else:
        sink_ref, q_ref, ck_ref, kn_ref, cv_ref, vn_ref, o_ref, kc_ref, vc_ref = refs
        pad = jnp.zeros(kn_ref.shape, F32)
        kk = jnp.concatenate([ck_ref[0], kn_ref[...], pad], axis=0)
        vv = jnp.concatenate([cv_ref[0], vn_ref[...], pad], axis=0)
        first_chunk = 0
        back = ck_ref.shape[1]
        kc_ref[0] = kk[tq:tq + back]
        vc_ref[0] = vv[tq:tq + back]
    nk = kk.shape[0]
    lanes = 128
    per = lanes // hd
    assert per == 2 and group % per == 0 and nk % lanes == 0
    qc = lax.broadcasted_iota(jnp.int32, (tq, per * nk), 0) // CHUNK
    kc = (lax.broadcasted_iota(jnp.int32, (tq, per * nk), 1) % nk) // CHUNK
    bias = jnp.where(kc >= jnp.maximum(qc, first_chunk), jnp.where(kc <= qc + N_BACK, 0.0, NEG), NEG)
    lane = lax.broadcasted_iota(jnp.int32, (nk, lanes), 1)
    ones_a = jnp.where(lane < hd, 1.0, 0.0)
    ones_b = jnp.where(lane >= hd, 1.0, 0.0)
    out_lane = lax.broadcasted_iota(jnp.int32, (tq, lanes), 1)
    for kvh in range(n_kv):
        tile, half = divmod(kvh, per)
        keep = (lane < hd) if half == 0 else (lane >= hd)

        def both_halves(x):
            own = jnp.where(keep, x[:, tile * lanes:(tile + 1) * lanes], 0.0)
            other = pltpu.roll(own, hd, axis=1)
            return (own, other) if half == 0 else (other, own)

        k_a, k_b = both_halves(kk * (hd ** -0.5))
        v_a, v_b = both_halves(vv)
        kbd = jnp.concatenate([k_a, k_b], axis=0).astype(BF16)
        vobd = jnp.concatenate([jnp.concatenate([v_a, ones_a], axis=1),
                                jnp.concatenate([v_b, ones_b], axis=1)], axis=0).astype(BF16)
        for pr in range(group // per):
            head_a = kvh * group + pr * per
            col0 = head_a * hd
            s = lax.dot_general(q_ref[:, col0:col0 + lanes], kbd, (((1,), (1,)), ((), ())),
                                preferred_element_type=F32) + bias
            sink_a, sink_b = sink_ref[head_a], sink_ref[head_a + 1]
            m_a = jnp.maximum(jnp.max(s[:, :nk], axis=-1, keepdims=True), sink_a)
            m_b = jnp.maximum(jnp.max(s[:, nk:], axis=-1, keepdims=True), sink_b)
            p = jnp.concatenate([jnp.exp(s[:, :nk] - m_a), jnp.exp(s[:, nk:] - m_b)], axis=1).astype(BF16)
            r = _dot(p, vobd)
            sink_term = jnp.where(out_lane < hd, jnp.exp(sink_a - m_a), jnp.exp(sink_b - m_b))
            o_ref[:, col0:col0 + lanes] = (r[:, :lanes] / (r[:, lanes:] + sink_term)).astype(BF16)


def _swa(q, k, v, cache_k, cache_v, sinks, *, bp, sp, bs, ss, tq):
    t, d = q.shape
    kvw = k.shape[1]
    tp = bp * sp
    n_heads = sinks.shape[0]
    hd = d // n_heads
    n_kv = kvw // hd
    back = N_BACK * CHUNK
    assert tq == back and sp % tq == 0 and ss == CHUNK and cache_k.shape[1] == back
    smem = pl.BlockSpec(memory_space=pltpu.SMEM)
    nj = sp // tq
    kw = dict(n_kv=n_kv, group=n_heads // n_kv, hd=hd)
    prev_blk = pl.BlockSpec((tq, kvw), lambda b, j: (b * nj + jnp.maximum(j - 1, 0), 0))
    this_blk = pl.BlockSpec((tq, kvw), lambda b, j: (b * nj + j, 0))
    o_p = pl.pallas_call(
        functools.partial(_swa_kernel, prompt=True, tq=tq, **kw),
        grid=(bp, nj),
        in_specs=[smem, pl.BlockSpec((tq, d), lambda b, j: (b * nj + j, 0)), prev_blk, this_blk, prev_blk, this_blk],
        out_specs=pl.BlockSpec((tq, d), lambda b, j: (b * nj + j, 0)),
        out_shape=jax.ShapeDtypeStruct((tp, d), BF16),
        compiler_params=_params(2),
        name="swa_prompt",
    )(sinks, q, k, k, v, v)
    off = tp // ss
    cache_blk = pl.BlockSpec((1, back, kvw), lambda b: (b, 0, 0))
    new_blk = pl.BlockSpec((ss, kvw), lambda b: (off + b, 0))
    o_s, kc_s, vc_s = pl.pallas_call(
        functools.partial(_swa_kernel, prompt=False, tq=ss, **kw),
        grid=(bs,),
        in_specs=[smem, pl.BlockSpec((ss, d), lambda b: (off + b, 0)), cache_blk, new_blk, cache_blk, new_blk],
        out_specs=[pl.BlockSpec((ss, d), lambda b: (b, 0)), cache_blk, cache_blk],
        out_shape=[jax.ShapeDtypeStruct((bs * ss, d), BF16), jax.ShapeDtypeStruct((bs, back, kvw), F32),
                   jax.ShapeDtypeStruct((bs, back, kvw), F32)],
        compiler_params=_params(1),
        name="swa_sample",
    )(sinks, q, cache_k, k, cache_v, v)
    return o_p, o_s, kc_s, vc_s


def _oproj_route_kernel(op_ref, os_ref, x3_ref, wo_ref, gffn_ref, wr_ref, br_ref, x4_ref, hm_ref, route_ref,
                        *, n_prompt_tiles):
    o = jnp.where(pl.program_id(0) < n_prompt_tiles, op_ref[...], os_ref[...])
    x4 = x3_ref[...] + _dot(o, wo_ref[...])
    x4_ref[...] = x4
    hm = _rms(x4, gffn_ref[...])
    hm_ref[...] = hm
    hm_hi = hm.astype(BF16)
    hm_lo = (hm - hm_hi.astype(F32)).astype(BF16)
    wr = wr_ref[...]
    wr_hi = wr.astype(BF16)
    wr_lo = (wr - wr_hi.astype(F32)).astype(BF16)
    logits = (_dot(hm_hi, wr_hi) + (_dot(hm_lo, wr_hi) + _dot(hm_hi, wr_lo))) + br_ref[...]
    lane = lax.broadcasted_iota(jnp.int32, logits.shape, 1).astype(F32)
    m1 = jnp.max(logits, axis=-1, keepdims=True)
    i1 = jnp.min(jnp.where(logits == m1, lane, float(ROUTE_LANES)), axis=-1, keepdims=True)
    rest = jnp.where(lane == i1, -jnp.inf, logits)
    m2 = jnp.max(rest, axis=-1, keepdims=True)
    i2 = jnp.min(jnp.where(rest == m2, lane, float(ROUTE_LANES)), axis=-1, keepdims=True)
    e2 = jnp.exp(m2 - m1)
    g1 = 1.0 / (1.0 + e2)
    g2 = e2 / (1.0 + e2)
    route = jnp.where(lane == 0.0, i1, jnp.where(lane == 1.0, i2,
                      jnp.where(lane == 2.0, g1, jnp.where(lane == 3.0, g2, 0.0))))
    route_ref[...] = route


def _oproj_route(o_p, o_s, x3, wo, gffn, wr, br, *, tm):
    t, d = x3.shape
    tp = o_p.shape[0]
    assert tp % tm == 0 and o_s.shape[0] % tm == 0
    row = lambda w: pl.BlockSpec((tm, w), lambda i: (i, 0))
    return pl.pallas_call(
        functools.partial(_oproj_route_kernel, n_prompt_tiles=tp // tm),
        grid=(t // tm,),
        in_specs=_split_specs(tm, d, tp // tm) + [row(d), _const_spec(wo.shape), pl.BlockSpec((1, d), lambda i: (0, 0)),
                  _const_spec(wr.shape), pl.BlockSpec((1, ROUTE_LANES), lambda i: (0, 0))],
        out_specs=[row(d), row(d), row(ROUTE_LANES)],
        out_shape=[jax.ShapeDtypeStruct((t, d), F32), jax.ShapeDtypeStruct((t, d), F32),
                   jax.ShapeDtypeStruct((t, ROUTE_LANES), F32)],
        compiler_params=_params(1),
        name="oproj_route",
    )(o_p, o_s, x3, wo, gffn, wr, br)


def _moe_kernel(texp_ref, first_ref, nreal_ref, rtok_ref, rdst_ref, hm_ref, w1_hbm, w3_hbm, w2_hbm, y_ref,
                xbuf, obuf, acc_ref, w1_ref, w3_ref, w2_ref, st13, st2, gsem, ssem, wsem,
                *, tm, fc):
    i = pl.program_id(0)
    n_slots = st2.shape[0]

    def gather_row(base, r, xb, sem):
        tok = rtok_ref[base + r]
        pltpu.make_async_copy(hm_ref.at[pl.ds(tok, 1), :], xb.at[pl.ds(r, 1), :], sem).start()

    def scatter_row(base, r, ob, sem):
        dst = rdst_ref[base + r]
        pltpu.make_async_copy(ob.at[pl.ds(r, 1), :], y_ref.at[pl.ds(dst, 1), :], sem).start()

    def weight_copies(c):
        e = texp_ref[i]
        slot = c % n_slots
        cols = pl.ds(c * fc, fc)
        return (pltpu.make_async_copy(w1_hbm.at[e, :, cols], st13.at[slot, 0], wsem.at[slot]),
                pltpu.make_async_copy(w3_hbm.at[e, :, cols], st13.at[slot, 1], wsem.at[slot]),
                pltpu.make_async_copy(w2_hbm.at[e, cols, :], st2.at[slot], wsem.at[slot]))

    def start_weights(c):
        for cp in weight_copies(c):
            cp.start(1)

    def stage_weights(c, n_chunks):
        for cp in weight_copies(c):
            cp.wait()
        slot = c % n_slots
        w1_ref[:, c * fc:(c + 1) * fc] = st13[slot, 0].astype(BF16)
        w3_ref[:, c * fc:(c + 1) * fc] = st13[slot, 1].astype(BF16)
        w2_ref[c * fc:(c + 1) * fc, :] = st2[slot].astype(BF16)
        if c + n_slots < n_chunks:
            start_weights(c + n_slots)

    def wait_gather(xb, sem):
        pltpu.make_async_copy(hm_ref.at[pl.ds(0, tm), :], xb, sem).wait()

    def wait_scatter(ob, sem):
        pltpu.make_async_copy(ob, y_ref.at[pl.ds(0, tm), :], sem).wait()

    @pl.when(i == 0)
    def _():
        obuf[...] = jnp.zeros(obuf.shape, F32)
        lax.fori_loop(0, tm, lambda r, c: (gather_row(0, r, xbuf.at[0], gsem.at[0]), c)[1], 0)

    is_first = first_ref[i] == 1

    @pl.when(is_first)
    def _():
        for c in range(n_slots):
            start_weights(c)

    p = i % 2
    xb, ob, xb_next, ob_prev = xbuf.at[p], obuf.at[p], xbuf.at[1 - p], obuf.at[1 - p]

    def step(streaming):
        wait_gather(xb, gsem.at[p])
        acc_ref[...] = jnp.zeros(acc_ref.shape, F32)
        g_base = (i + 1) * tm
        s_base = i * tm

        def move_rows(c, n_chunks):
            per = -(-2 * tm // n_chunks)
            for m in range(c * per, min((c + 1) * per, 2 * tm)):
                if m < tm:
                    gather_row(g_base, m, xb_next, gsem.at[1 - p])
                else:
                    scatter_row(s_base, m - tm, ob_prev, ssem.at[1 - p])

        _swiglu_acc(xb[...].astype(BF16), w1_ref, w3_ref, w2_ref, acc_ref, fc,
                    before_chunk=stage_weights if streaming else None, after_chunk=move_rows)

        @pl.when(i >= 1)
        def _():
            wait_scatter(ob, ssem.at[p])
        ob[...] = acc_ref[...]

    n_real = nreal_ref[0]

    @pl.when(is_first)
    def _():
        step(True)

    @pl.when(jnp.logical_and(i < n_real, jnp.logical_not(is_first)))
    def _():
        step(False)

    @pl.when(i == n_real)
    def _():
        wait_gather(xb, gsem.at[p])

        @pl.when(i >= 1)
        def _():
            wait_scatter(ob, ssem.at[p])
        lax.fori_loop(0, tm, lambda r, c: (scatter_row(i * tm, r, ob_prev, ssem.at[1 - p]), c)[1], 0)
        wait_scatter(ob_prev, ssem.at[1 - p])


def _moe(hm, route, w1, w3, w2, *, tm, fc):
    t, d = hm.shape
    n_exp = w1.shape[0]
    n_assign = t * TOP_K
    nt = n_assign // tm + n_exp
    assert n_assign % tm == 0
    e_flat = route[:, :TOP_K].astype(jnp.int32).reshape(-1)
    order = jnp.argsort(e_flat, stable=True).astype(jnp.int32)
    counts = jnp.sum((e_flat[:, None] == jnp.arange(n_exp, dtype=jnp.int32)[None, :]).astype(jnp.int32), axis=0)
    tiles_e = (counts + tm - 1) // tm
    tile_end = jnp.cumsum(tiles_e)
    tile_start = tile_end - tiles_e
    cnt_start = jnp.cumsum(counts) - counts
    tile_ids = jnp.arange(nt, dtype=jnp.int32)
    texp = jnp.minimum(jnp.sum((tile_ids[:, None] >= tile_end[None, :]).astype(jnp.int32), axis=1), n_exp - 1)
    within = (tile_ids - tile_start[texp]) * tm
    nval = jnp.where(tile_ids < tile_end[-1], jnp.clip(counts[texp] - within, 0, tm), 0).astype(jnp.int32)
    r_in = jnp.arange(tm, dtype=jnp.int32)[None, :]
    src = jnp.clip(cnt_start[texp][:, None] + within[:, None] + r_in, 0, n_assign - 1)
    real = r_in < nval[:, None]
    assign = jnp.where(real, order[src], 0)
    scratch_row = TOP_K * t + (tile_ids[:, None] % 2) * tm + r_in
    rtok = jnp.concatenate([(assign // TOP_K).reshape(-1), jnp.zeros((2 * tm,), jnp.int32)])
    rdst = jnp.concatenate([TOP_K * t + tm + r_in[0],
                            jnp.where(real, (assign % TOP_K) * t + assign // TOP_K, scratch_row).reshape(-1)])
    texp = jnp.concatenate([texp, texp[-1:]])
    prev_exp = jnp.concatenate([jnp.full((1,), -1, jnp.int32), texp[:-1]])
    first = jnp.where(jnp.arange(nt + 1) < tile_end[-1], texp != prev_exp, False).astype(jnp.int32)
    n_steps = nt + 1
    d_ff = w1.shape[2]
    n_slots = 2
    assert d_ff % fc == 0 and d_ff // fc >= n_slots

    hbm = pl.BlockSpec(memory_space=pl.ANY)
    y2 = pl.pallas_call(
        functools.partial(_moe_kernel, tm=tm, fc=fc),
        grid_spec=pltpu.PrefetchScalarGridSpec(
            num_scalar_prefetch=5,
            grid=(n_steps,),
            in_specs=[hbm, hbm, hbm, hbm],
            out_specs=hbm,
            scratch_shapes=[pltpu.VMEM((2, tm, d), F32), pltpu.VMEM((2, tm, d), F32), pltpu.VMEM((tm, d), F32),
                            pltpu.VMEM((d, d_ff), BF16), pltpu.VMEM((d, d_ff), BF16), pltpu.VMEM((d_ff, d), BF16),
               pltpu.VMEM((n_slots, 2, d, fc), F32), pltpu.VMEM((n_slots, fc, d), F32),
               pltpu.SemaphoreType.DMA((2,)), pltpu.SemaphoreType.DMA((2,)), pltpu.SemaphoreType.DMA((n_slots,))],
        ),
        out_shape=jax.ShapeDtypeStruct((TOP_K * t + 2 * tm, d), F32),
        compiler_params=_params(1),
        name="moe",
    )(texp, first, tile_end[-1:].astype(jnp.int32), rtok, rdst, hm, w1, w3, w2)
    return y2


def _final_kernel(x4_ref, ya_ref, yb_ref, route_ref, pp_ref, ps_ref, gple_ref, wproj_ref, wgate_ref, gfin_ref,
                  yp_ref, ys_ref, *, n_prompt_tiles):
    i = pl.program_id(0)
    route = route_ref[...]
    g1 = route[:, 2:3]
    g2 = route[:, 3:4]
    x5 = x4_ref[...] + (g1 * ya_ref[...] + g2 * yb_ref[...])
    p = jnp.where(i < n_prompt_tiles, pp_ref[...], ps_ref[...]).astype(BF16)
    x6 = _ple_gate(x5, p, gple_ref[...], wproj_ref, wgate_ref)
    y = _rms(x6, gfin_ref[...])

    @pl.when(i < n_prompt_tiles)
    def _():
        yp_ref[...] = y

    @pl.when(i >= n_prompt_tiles)
    def _():
        ys_ref[...] = y


def _final(x4, y2, route, pp, ps, gple, wproj, wgate, gfin, *, tm):
    t, d = x4.shape
    tp, ts = pp.shape[1], ps.shape[1]
    npt = tp // tm
    assert tp % tm == 0 and ts % tm == 0
    row = lambda w: pl.BlockSpec((tm, w), lambda i: (i, 0))
    vec = pl.BlockSpec((1, d), lambda i: (0, 0))
    return pl.pallas_call(
        functools.partial(_final_kernel, n_prompt_tiles=npt),
        grid=(t // tm,),
        in_specs=[row(d), row(d), pl.BlockSpec((tm, d), lambda i: (t // tm + i, 0)), row(ROUTE_LANES)]
        + _split_specs(tm, pp.shape[2], npt, layer=1) + [vec, _const_spec(wproj.shape), _const_spec(wgate.shape), vec],
        out_specs=_split_specs(tm, d, npt),
        out_shape=[jax.ShapeDtypeStruct((tp, d), F32), jax.ShapeDtypeStruct((ts, d), F32)],
        compiler_params=_params(1),
        name="final",
    )(x4, y2, y2, route, pp, ps, gple, wproj, wgate, gfin)


def kernel(x_prompt, x_sample, state_pool, cache_k, cache_v, p_prompt, p_sample, g_mix, g_ffn, g_ple, g_kv, g_final,
           w_pool, pool_scale, w_q, b_q, w_k, b_k, w_v, b_v, w_o, sinks, w1_dense, w3_dense, w2_dense, w_router,
           b_router, w1_moe, w3_moe, w2_moe, w_ple_proj, w_ple_gate):
    bp, sp, d = x_prompt.shape
    bs, ss, _ = x_sample.shape
    depth = g_mix.shape[0]
    assert depth == 2 and w_pool.shape[0] == 1 and w_q.shape[0] == 1 and w1_dense.shape[0] == 1 and w1_moe.shape[0] == 1
    tp, ts = bp * sp, bs * ss
    n_kv, hd = cache_k.shape[2], cache_k.shape[3]
    n_exp = w_router.shape[2]
    ple = p_prompt.shape[-1]
    bf = lambda w: w.astype(BF16)
    vec = lambda g: g.reshape(1, -1)
    pp = p_prompt.reshape(depth, tp, ple)
    ps = p_sample.reshape(depth, ts, ple)

    x1_p, x1_s, hf_p, hf_s, pool_p, pool_s = _pool_front(
        x_prompt, x_sample, state_pool[0], vec(g_mix[0]), vec(g_ffn[0]), bf(w_pool[0]), vec(pool_scale[0]),
        ta=512, nbs=8)

    back = cache_k.shape[1]
    ck = cache_k.reshape(bs, back, n_kv * hd)
    cv = cache_v.reshape(bs, back, n_kv * hd)
    x3, k, v, q, kc_p, vc_p = _ffn0_post(
        hf_p, hf_s, x1_p, x1_s, pp, ps,
        bf(w1_dense[0]), bf(w3_dense[0]), bf(w2_dense[0]),
        vec(g_ple[0]), bf(w_ple_proj[0]), bf(w_ple_gate[0]),
        vec(g_kv), bf(w_k), vec(b_k), bf(w_v), vec(b_v),
        vec(g_mix[1]), bf(w_q[0]), vec(b_q[0]), tm=512, fc=512, sp=sp, back=back)

    o_p, o_s, kc_s, vc_s = _swa(q, k, v, ck, cv, sinks[0], bp=bp, sp=sp, bs=bs, ss=ss, tq=N_BACK * CHUNK)

    wr = jnp.pad(w_router[0], ((0, 0), (0, ROUTE_LANES - n_exp)))
    br = jnp.pad(b_router[0], (0, ROUTE_LANES - n_exp), constant_values=NEG).reshape(1, ROUTE_LANES)
    x4, hm, route = _oproj_route(o_p, o_s, x3, bf(w_o[0]), vec(g_ffn[1]), wr, br, tm=512)

    y2 = _moe(hm, route, w1_moe[0], w3_moe[0], w2_moe[0], tm=512, fc=512)

    y_p, y_s = _final(x4, y2, route, pp, ps,
                      vec(g_ple[1]), bf(w_ple_proj[1]), bf(w_ple_gate[1]), vec(g_final), tm=512)

    heads = lambda c: c.reshape(c.shape[0], back, n_kv, hd)
    return (y_p.reshape(bp, sp, d), y_s.reshape(bs, ss, d), pool_p, pool_s,
            heads(kc_p), heads(vc_p), heads(kc_s), heads(vc_s))
```

```python
import functools

import jax
import jax.numpy as jnp
from jax import lax
from jax.experimental import pallas as pl
from jax.experimental.pallas import tpu as pltpu

EPS = 1e-6
NEG = -1e30
CHUNK = 64
N_BACK = 2
POOL_WINDOWS = (2, 4, 8, 16)
POOL_BUF = 15
HIST = 16
TOP_K = 2
ROUTE_LANES = 128
VMEM_LIMIT = 56 * 1024 * 1024

F32 = jnp.float32
BF16 = jnp.bfloat16


def _rms(x, g):
    return (x * lax.rsqrt(jnp.mean(x * x, axis=-1, keepdims=True) + EPS)) * g


def _dot(a, b):
    return jnp.dot(a, b, preferred_element_type=F32)


def _const_spec(shape):
    nd = len(shape)
    return pl.BlockSpec(shape, lambda *_: (0,) * nd, pipeline_mode=pl.Buffered(1))


def _params(n_axes):
    return pltpu.CompilerParams(dimension_semantics=("arbitrary",) * n_axes, vmem_limit_bytes=VMEM_LIMIT)


def _pool_front_kernel(*refs, prompt, nb, ta, d):
    (x_ref, hist_ref, gmix_ref, gffn_ref, wpool_ref, scale_ref,
     x1_ref, hf_ref, state_ref, e_ref, f2_ref, f4_ref, f8_ref) = refs
    gw = d // len(POOL_WINDOWS)
    j = pl.program_id(1)
    n = ta + HIST
    x = x_ref[...]
    g = gmix_ref[...]
    h = _rms(x, g)
    if prompt:
        hp = _rms(hist_ref[...], g)
        hp = jnp.where(j == 0, 0.0, hp)
        e_ref[:, 0:HIST, :] = hp
    else:
        e_ref[:, 0:1, :] = jnp.zeros((nb, 1, d), F32)
        e_ref[:, 1:HIST, :] = hist_ref[...]
    e_ref[:, HIST:n, :] = h
    zero_tail = jnp.zeros((nb, 8, d), F32)
    e_ref[:, n:n + 8, :] = zero_tail
    f2_ref[:, 0:n, :] = e_ref[:, 0:n, :] + e_ref[:, 1:n + 1, :]
    f2_ref[:, n:n + 8, :] = zero_tail
    f4_ref[:, 0:n, :] = f2_ref[:, 0:n, gw:] + f2_ref[:, 2:n + 2, gw:]
    f4_ref[:, n:n + 8, :] = zero_tail[:, :, gw:]
    f8_ref[:, 0:n, :] = f4_ref[:, 0:n, gw:] + f4_ref[:, 4:n + 4, gw:]
    f8_ref[:, n:n + 8, :] = zero_tail[:, :, 2 * gw:]
    sums = (
        f2_ref[:, HIST - 1:HIST - 1 + ta, 0:gw],
        f4_ref[:, HIST - 3:HIST - 3 + ta, 0:gw],
        f8_ref[:, HIST - 7:HIST - 7 + ta, 0:gw],
        f8_ref[:, HIST - 15:HIST - 15 + ta, gw:] + f8_ref[:, HIST - 7:HIST - 7 + ta, gw:],
    )
    t_idx = lax.broadcasted_iota(jnp.int32, (1, ta, 1), 1) + j * ta
    ys = []
    for gi, w in enumerate(POOL_WINDOWS):
        if prompt:
            cnt = jnp.minimum(t_idx + 1, w).astype(F32)
            mean = sums[gi] / cnt
        else:
            mean = sums[gi] * (1.0 / w)
        diff = mean - h[:, :, gi * gw:(gi + 1) * gw]
        ys.append(_dot(diff.reshape(nb * ta, gw).astype(BF16), wpool_ref[gi]))
    y = jnp.concatenate(ys, axis=-1) * scale_ref[...]
    x1 = x.reshape(nb * ta, d) + y
    x1_ref[...] = x1
    hf_ref[...] = _rms(x1, gffn_ref[...]).astype(BF16)
    if prompt:
        @pl.when(j == pl.num_programs(1) - 1)
        def _():
            state_ref[0] = e_ref[:, n - POOL_BUF:n, :]
    else:
        state_ref[0] = e_ref[:, n - POOL_BUF:n, :]


def _pool_front(x_p, x_s, state, gmix, gffn, wpool, scale, *, ta, nbs):
    bp, sp, d = x_p.shape
    bs, ss, _ = x_s.shape
    tp, ts = bp * sp, bs * ss
    gw = d // len(POOL_WINDOWS)
    assert sp % ta == 0 and ta % HIST == 0 and bs % nbs == 0
    vec = pl.BlockSpec((1, d), lambda *_: (0, 0))
    wspec = pl.BlockSpec((len(POOL_WINDOWS), gw, gw), lambda *_: (0, 0, 0))

    def scratch(nb, rows):
        return [pltpu.VMEM((nb, rows + HIST + 8, d), F32), pltpu.VMEM((nb, rows + HIST + 8, d), F32),
                pltpu.VMEM((nb, rows + HIST + 8, d - gw), F32), pltpu.VMEM((nb, rows + HIST + 8, d - 2 * gw), F32)]

    njp = sp // ta
    x1_p, hf_p, state_p = pl.pallas_call(
        functools.partial(_pool_front_kernel, prompt=True, nb=1, ta=ta, d=d),
        grid=(bp, njp),
        in_specs=[
            pl.BlockSpec((1, ta, d), lambda b, j: (b, j, 0)),
            pl.BlockSpec((1, HIST, d), lambda b, j: (b, jnp.maximum(j * (ta // HIST) - 1, 0), 0)),
            vec, vec, wspec, vec,
        ],
        out_specs=[
            pl.BlockSpec((ta, d), lambda b, j: (b * njp + j, 0)),
            pl.BlockSpec((ta, d), lambda b, j: (b * njp + j, 0)),
            pl.BlockSpec((1, 1, POOL_BUF, d), lambda b, j: (0, b, 0, 0)),
        ],
        out_shape=[jax.ShapeDtypeStruct((tp, d), F32), jax.ShapeDtypeStruct((tp, d), BF16),
                   jax.ShapeDtypeStruct((1, bp, POOL_BUF, d), F32)],
        scratch_shapes=scratch(1, ta),
        compiler_params=_params(2),
        name="pool_front_prompt",
    )(x_p, x_p, gmix, gffn, wpool, scale)

    rows = nbs * ss
    x1_s, hf_s, state_s = pl.pallas_call(
        functools.partial(_pool_front_kernel, prompt=False, nb=nbs, ta=ss, d=d),
        grid=(bs // nbs, 1),
        in_specs=[
            pl.BlockSpec((nbs, ss, d), lambda b, j: (b, 0, 0)),
            pl.BlockSpec((nbs, POOL_BUF, d), lambda b, j: (b, 0, 0)),
            vec, vec, wspec, vec,
        ],
        out_specs=[
            pl.BlockSpec((rows, d), lambda b, j: (b, 0)),
            pl.BlockSpec((rows, d), lambda b, j: (b, 0)),
            pl.BlockSpec((1, nbs, POOL_BUF, d), lambda b, j: (0, b, 0, 0)),
        ],
        out_shape=[jax.ShapeDtypeStruct((ts, d), F32), jax.ShapeDtypeStruct((ts, d), BF16),
                   jax.ShapeDtypeStruct((1, bs, POOL_BUF, d), F32)],
        scratch_shapes=scratch(nbs, ss),
        compiler_params=_params(2),
        name="pool_front_sample",
    )(x_s, state, gmix, gffn, wpool, scale)
    return x1_p, x1_s, hf_p, hf_s, state_p, state_s


def _swiglu_acc(x_bf, w1_ref, w3_ref, w2_ref, acc_ref, fc, before_chunk=None, after_chunk=None):
    d_ff = w1_ref.shape[-1]
    n_chunks = d_ff // fc
    for c in range(n_chunks):
        if before_chunk is not None:
            before_chunk(c, n_chunks)
        sl = slice(c * fc, (c + 1) * fc)
        h1 = _dot(x_bf, w1_ref[:, sl])
        h3 = _dot(x_bf, w3_ref[:, sl])
        a = (h1 * jax.nn.sigmoid(h1)) * h3
        acc_ref[...] += _dot(a.astype(BF16), w2_ref[sl, :])
        if after_chunk is not None:
            after_chunk(c, n_chunks)


def _ple_gate(x, p_bf, gple, wproj_ref, wgate_ref):
    e = _dot(p_bf, wproj_ref[...])
    gate = jax.nn.sigmoid(_dot(_rms(x, gple).astype(BF16), wgate_ref[...]))
    return x + gate * e


def _ffn0_post_kernel(hfp_ref, hfs_ref, x1p_ref, x1s_ref, pp_ref, ps_ref, w1_ref, w3_ref, w2_ref,
                      gple_ref, wproj_ref, wgate_ref, gkv_ref, wk_ref, bk_ref, wv_ref, bv_ref,
                      gmix_ref, wq_ref, bq_ref,
                      x3_ref, k_ref, v_ref, q_ref, kcp_ref, vcp_ref, acc_ref,
                      *, fc, n_prompt_tiles, tiles_per_seq):
    i = pl.program_id(0)
    is_prompt = i < n_prompt_tiles
    acc_ref[...] = jnp.where(is_prompt, x1p_ref[...], x1s_ref[...])
    _swiglu_acc(jnp.where(is_prompt, hfp_ref[...], hfs_ref[...]), w1_ref, w3_ref, w2_ref, acc_ref, fc)
    p = jnp.where(is_prompt, pp_ref[...], ps_ref[...]).astype(BF16)
    x3 = _ple_gate(acc_ref[...], p, gple_ref[...], wproj_ref, wgate_ref)
    x3_ref[...] = x3
    kv_in = _rms(x3, gkv_ref[...]).astype(BF16)
    k = _dot(kv_in, wk_ref[...]) + bk_ref[...]
    v = _dot(kv_in, wv_ref[...]) + bv_ref[...]
    k_ref[...] = k
    v_ref[...] = v
    hq = _rms(x3, gmix_ref[...]).astype(BF16)
    q_ref[...] = (_dot(hq, wq_ref[...]) + bq_ref[...]).astype(BF16)

    back = kcp_ref.shape[1]

    @pl.when(jnp.logical_and(is_prompt, i % tiles_per_seq == tiles_per_seq - 1))
    def _():
        kcp_ref[0] = k[k.shape[0] - back:]
        vcp_ref[0] = v[v.shape[0] - back:]


def _split_specs(tm, width, n_prompt_tiles, layer=None, lean=False):
    kw = dict(pipeline_mode=pl.Buffered(1)) if lean else {}
    if layer is None:
        return [pl.BlockSpec((tm, width), lambda i, *_: (jnp.minimum(i, n_prompt_tiles - 1), 0)),
                pl.BlockSpec((tm, width), lambda i, *_: (jnp.maximum(i - n_prompt_tiles, 0), 0), **kw)]
    return [pl.BlockSpec((None, tm, width), lambda i, *_: (layer, jnp.minimum(i, n_prompt_tiles - 1), 0)),
            pl.BlockSpec((None, tm, width), lambda i, *_: (layer, jnp.maximum(i - n_prompt_tiles, 0), 0), **kw)]


def _ffn0_post(hf_p, hf_s, x1_p, x1_s, pp, ps, w1, w3, w2, gple, wproj, wgate, gkv, wk, bk, wv, bv,
               gmix, wq, bq, *, tm, fc, sp, back):
    tp, d = x1_p.shape
    ts = x1_s.shape[0]
    t = tp + ts
    kvw = wk.shape[1]
    bp = tp // sp
    assert ts % tm == 0 and tp % tm == 0 and w1.shape[1] % fc == 0 and sp % tm == 0 and back <= tm
    npt = tp // tm
    row = lambda w: pl.BlockSpec((tm, w), lambda i: (i, 0))
    vec = lambda w: pl.BlockSpec((1, w), lambda i: (0, 0))
    pcache = pl.BlockSpec((1, back, kvw), lambda i: (jnp.minimum(i // (sp // tm), bp - 1), 0, 0))
    cache_shape = jax.ShapeDtypeStruct((bp, back, kvw), F32)
    return pl.pallas_call(
        functools.partial(_ffn0_post_kernel, fc=fc, n_prompt_tiles=npt, tiles_per_seq=sp // tm),
        grid=(t // tm,),
        in_specs=_split_specs(tm, d, npt) + _split_specs(tm, d, npt) + _split_specs(tm, pp.shape[2], npt, layer=0) + [
            _const_spec(w1.shape), _const_spec(w3.shape), _const_spec(w2.shape),
            vec(d), _const_spec(wproj.shape), _const_spec(wgate.shape),
            vec(d), _const_spec(wk.shape), vec(kvw), _const_spec(wv.shape), vec(kvw),
            vec(d), _const_spec(wq.shape), vec(d),
        ],
        out_specs=[row(d), row(kvw), row(kvw), row(d), pcache, pcache],
        out_shape=[jax.ShapeDtypeStruct((t, d), F32), jax.ShapeDtypeStruct((t, kvw), F32),
                   jax.ShapeDtypeStruct((t, kvw), F32), jax.ShapeDtypeStruct((t, d), BF16), cache_shape, cache_shape],
        scratch_shapes=[pltpu.VMEM((tm, d), F32)],
        compiler_params=_params(1),
        name="ffn0_post",
    )(hf_p, hf_s, x1_p, x1_s, pp, ps, w1, w3, w2, gple, wproj, wgate, gkv, wk, bk, wv, bv, gmix, wq, bq)


def _swa_kernel(*refs, prompt, tq, n_kv, group, hd):
    if prompt:
        sink_ref, q_ref, ka_ref, kb_ref, va_ref, vb_ref, o_ref = refs
        kk = jnp.concatenate([ka_ref[...], kb_ref[...]], axis=0)
        vv = jnp.concatenate([va_ref[...], vb_ref[...]], axis=0)
        first_chunk = N_BACK - pl.program_id(1) * (tq // CHUNK)
    else:
        sink_ref, q_ref, ck_ref, kn_ref, cv_ref, vn_ref, o_ref, kc_ref, vc_ref = refs
        pad = jnp.zeros(kn_ref.shape, F32)
        kk = jnp.concatenate([ck_ref[0], kn_ref[...], pad], axis=0)
        vv = jnp.concatenate([cv_ref[0], vn_ref[...], pad], axis=0)
        first_chunk = 0
        back = ck_ref.shape[1]
        kc_ref[0] = kk[tq:tq + back]
        vc_ref[0] = vv[tq:tq + back]
    nk = kk.shape[0]
    lanes = 128
    per = lanes // hd
    assert per == 2 and group % per == 0 and nk % lanes == 0
    qc = lax.broadcasted_iota(jnp.int32, (tq, per * nk), 0) // CHUNK
    kc = (lax.broadcasted_iota(jnp.int32, (tq, per * nk), 1) % nk) // CHUNK
    bias = jnp.where(kc >= jnp.maximum(qc, first_chunk), jnp.where(kc <= qc + N_BACK, 0.0, NEG), NEG)
    lane = lax.broadcasted_iota(jnp.int32, (nk, lanes), 1)
    ones_a = jnp.where(lane < hd, 1.0, 0.0)
    ones_b = jnp.where(lane >= hd, 1.0, 0.0)
    out_lane = lax.broadcasted_iota(jnp.int32, (tq, lanes), 1)
    for kvh in range(n_kv):
        tile, half = divmod(kvh, per)
        keep = (lane < hd) if half == 0 else (lane >= hd)

        def both_halves(x):
            own = jnp.where(keep, x[:, tile * lanes:(tile + 1) * lanes], 0.0)
            other = pltpu.roll(own, hd, axis=1)
            return (own, other) if half == 0 else (other, own)

        k_a, k_b = both_halves(kk * (hd ** -0.5))
        v_a, v_b = both_halves(vv)
        kbd = jnp.concatenate([k_a, k_b], axis=0).astype(BF16)
        vobd = jnp.concatenate([jnp.concatenate([v_a, ones_a], axis=1),
                                jnp.concatenate([v_b, ones_b], axis=1)], axis=0).astype(BF16)
        for pr in range(group // per):
            head_a = kvh * group + pr * per
            col0 = head_a * hd
            s = lax.dot_general(q_ref[:, col0:col0 + lanes], kbd, (((1,), (1,)), ((), ())),
                                preferred_element_type=F32) + bias
            sink_a, sink_b = sink_ref[head_a], sink_ref[head_a + 1]
            m_a = jnp.maximum(jnp.max(s[:, :nk], axis=-1, keepdims=True), sink_a)
            m_b = jnp.maximum(jnp.max(s[:, nk:], axis=-1, keepdims=True), sink_b)
            p = jnp.concatenate([jnp.exp(s[:, :nk] - m_a), jnp.exp(s[:, nk:] - m_b)], axis=1).astype(BF16)
            r = _dot(p, vobd)
            sink_term = jnp.where(out_lane < hd, jnp.exp(sink_a - m_a), jnp.exp(sink_b - m_b))
            o_ref[:, col0:col0 + lanes] = (r[:, :lanes] / (r[:, lanes:] + sink_term)).astype(BF16)


def _swa(q, k, v, cache_k, cache_v, sinks, *, bp, sp, bs, ss, tq):
    t, d = q.shape
    kvw = k.shape[1]
    tp = bp * sp
    n_heads = sinks.shape[0]
    hd = d // n_heads
    n_kv = kvw // hd
    back = N_BACK * CHUNK
    assert tq == back and sp % tq == 0 and ss == CHUNK and cache_k.shape[1] == back
    smem = pl.BlockSpec(memory_space=pltpu.SMEM)
    nj = sp // tq
    kw = dict(n_kv=n_kv, group=n_heads // n_kv, hd=hd)
    prev_blk = pl.BlockSpec((tq, kvw), lambda b, j: (b * nj + jnp.maximum(j - 1, 0), 0))
    this_blk = pl.BlockSpec((tq, kvw), lambda b, j: (b * nj + j, 0))
    o_p = pl.pallas_call(
        functools.partial(_swa_kernel, prompt=True, tq=tq, **kw),
        grid=(bp, nj),
        in_specs=[smem, pl.BlockSpec((tq, d), lambda b, j: (b * nj + j, 0)), prev_blk, this_blk, prev_blk, this_blk],
        out_specs=pl.BlockSpec((tq, d), lambda b, j: (b * nj + j, 0)),
        out_shape=jax.ShapeDtypeStruct((tp, d), BF16),
        compiler_params=_params(2),
        name="swa_prompt",
    )(sinks, q, k, k, v, v)
    off = tp // ss
    cache_blk = pl.BlockSpec((1, back, kvw), lambda b: (b, 0, 0))
    new_blk = pl.BlockSpec((ss, kvw), lambda b: (off + b, 0))
    o_s, kc_s, vc_s = pl.pallas_call(
        functools.partial(_swa_kernel, prompt=False, tq=ss, **kw),
        grid=(bs,),
        in_specs=[smem, pl.BlockSpec((ss, d), lambda b: (off + b, 0)), cache_blk, new_blk, cache_blk, new_blk],
        out_specs=[pl.BlockSpec((ss, d), lambda b: (b, 0)), cache_blk, cache_blk],
        out_shape=[jax.ShapeDtypeStruct((bs * ss, d), BF16), jax.ShapeDtypeStruct((bs, back, kvw), F32),
                   jax.ShapeDtypeStruct((bs, back, kvw), F32)],
        compiler_params=_params(1),
        name="swa_sample",
    )(sinks, q, cache_k, k, cache_v, v)
    return o_p, o_s, kc_s, vc_s


def _oproj_route_kernel(op_ref, os_ref, x3_ref, wo_ref, gffn_ref, wr_ref, br_ref, x4_ref, hm_ref, route_ref,
                        *, n_prompt_tiles):
    o = jnp.where(pl.program_id(0) < n_prompt_tiles, op_ref[...], os_ref[...])
    x4 = x3_ref[...] + _dot(o, wo_ref[...])
    x4_ref[...] = x4
    hm = _rms(x4, gffn_ref[...])
    hm_ref[...] = hm
    hm_hi = hm.astype(BF16)
    hm_lo = (hm - hm_hi.astype(F32)).astype(BF16)
    wr = wr_ref[...]
    wr_hi = wr.astype(BF16)
    wr_lo = (wr - wr_hi.astype(F32)).astype(BF16)
    hi_both = _dot(hm_hi, jnp.concatenate([wr_hi, wr_lo], axis=1))
    logits = (hi_both[:, :ROUTE_LANES] + (_dot(hm_lo, wr_hi) + hi_both[:, ROUTE_LANES:])) + br_ref[...]
    lane = lax.broadcasted_iota(jnp.int32, logits.shape, 1).astype(F32)
    m1 = jnp.max(logits, axis=-1, keepdims=True)
    i1 = jnp.min(jnp.where(logits == m1, lane, float(ROUTE_LANES)), axis=-1, keepdims=True)
    rest = jnp.where(lane == i1, -jnp.inf, logits)
    m2 = jnp.max(rest, axis=-1, keepdims=True)
    i2 = jnp.min(jnp.where(rest == m2, lane, float(ROUTE_LANES)), axis=-1, keepdims=True)
    e2 = jnp.exp(m2 - m1)
    g1 = 1.0 / (1.0 + e2)
    g2 = e2 / (1.0 + e2)
    route = jnp.where(lane == 0.0, i1, jnp.where(lane == 1.0, i2,
                      jnp.where(lane == 2.0, g1, jnp.where(lane == 3.0, g2, 0.0))))
    route_ref[...] = route


def _oproj_route(o_p, o_s, x3, wo, gffn, wr, br, *, tm):
    t, d = x3.shape
    tp = o_p.shape[0]
    assert tp % tm == 0 and o_s.shape[0] % tm == 0
    row = lambda w: pl.BlockSpec((tm, w), lambda i: (i, 0))
    return pl.pallas_call(
        functools.partial(_oproj_route_kernel, n_prompt_tiles=tp // tm),
        grid=(t // tm,),
        in_specs=_split_specs(tm, d, tp // tm) + [row(d), _const_spec(wo.shape), pl.BlockSpec((1, d), lambda i: (0, 0)),
                  _const_spec(wr.shape), pl.BlockSpec((1, ROUTE_LANES), lambda i: (0, 0))],
        out_specs=[row(d), row(d), row(ROUTE_LANES)],
        out_shape=[jax.ShapeDtypeStruct((t, d), F32), jax.ShapeDtypeStruct((t, d), F32),
                   jax.ShapeDtypeStruct((t, ROUTE_LANES), F32)],
        compiler_params=_params(1),
        name="oproj_route",
    )(o_p, o_s, x3, wo, gffn, wr, br)


def _moe_kernel(texp_ref, first_ref, nreal_ref, rtok_ref, rdst_ref, hm_ref, w1_hbm, w3_hbm, w2_hbm, y_ref,
                xbuf, obuf, acc_ref, w1_ref, w3_ref, w2_ref, st13, st2, gsem, ssem, wsem,
                *, tm, fc):
    i = pl.program_id(0)
    n_slots = st2.shape[0]

    def gather_row(base, r, xb, sem):
        tok = rtok_ref[base + r]
        pltpu.make_async_copy(hm_ref.at[pl.ds(tok, 1), :], xb.at[pl.ds(r, 1), :], sem).start()

    def scatter_row(base, r, ob, sem):
        dst = rdst_ref[base + r]
        pltpu.make_async_copy(ob.at[pl.ds(r, 1), :], y_ref.at[pl.ds(dst, 1), :], sem).start()

    def weight_copies(c, step=None):
        e = texp_ref[i if step is None else step]
        slot = c % n_slots
        cols = pl.ds(c * fc, fc)
        return (pltpu.make_async_copy(w1_hbm.at[e, :, cols], st13.at[slot, 0], wsem.at[slot]),
                pltpu.make_async_copy(w3_hbm.at[e, :, cols], st13.at[slot, 1], wsem.at[slot]),
                pltpu.make_async_copy(w2_hbm.at[e, cols, :], st2.at[slot], wsem.at[slot]))

    def start_weights(c, step=None):
        for cp in weight_copies(c, step):
            cp.start(1)

    def stage_weights(c, n_chunks):
        for cp in weight_copies(c):
            cp.wait()
        slot = c % n_slots
        w1_ref[:, c * fc:(c + 1) * fc] = st13[slot, 0].astype(BF16)
        w3_ref[:, c * fc:(c + 1) * fc] = st13[slot, 1].astype(BF16)
        w2_ref[c * fc:(c + 1) * fc, :] = st2[slot].astype(BF16)
        if c + n_slots < n_chunks:
            start_weights(c + n_slots)

    def wait_gather(xb, sem):
        pltpu.make_async_copy(hm_ref.at[pl.ds(0, tm), :], xb, sem).wait()

    def wait_scatter(ob, sem):
        pltpu.make_async_copy(ob, y_ref.at[pl.ds(0, tm), :], sem).wait()

    @pl.when(i == 0)
    def _():
        obuf[...] = jnp.zeros(obuf.shape, F32)
        lax.fori_loop(0, tm, lambda r, c: (gather_row(0, r, xbuf.at[0], gsem.at[0]), c)[1], 0)
        n_out = y_ref.shape[0] - 2 * tm
        fills = [pltpu.make_async_copy(obuf.at[s], y_ref.at[pl.ds(n_out + s * tm, tm), :], ssem.at[s])
                 for s in range(2)]
        for cp in fills:
            cp.start()
        for cp in fills:
            cp.wait()

    is_first = first_ref[i] == 1
    prev_first = jnp.logical_or(i == 0, first_ref[jnp.maximum(i - 1, 0)] == 1)

    @pl.when(jnp.logical_and(is_first, prev_first))
    def _():
        for c in range(n_slots):
            start_weights(c)

    @pl.when(jnp.logical_and(first_ref[i + 1] == 1, jnp.logical_not(is_first)))
    def _():
        for c in range(n_slots):
            start_weights(c, i + 1)

    p = i % 2
    xb, ob, xb_next, ob_prev = xbuf.at[p], obuf.at[p], xbuf.at[1 - p], obuf.at[1 - p]

    def step(streaming):
        wait_gather(xb, gsem.at[p])
        acc_ref[...] = jnp.zeros(acc_ref.shape, F32)
        g_base = (i + 1) * tm
        s_base = i * tm

        def move_rows(c, n_chunks):
            per = -(-2 * tm // n_chunks)
            for m in range(c * per, min((c + 1) * per, 2 * tm)):
                if m < tm:
                    gather_row(g_base, m, xb_next, gsem.at[1 - p])
                else:
                    scatter_row(s_base, m - tm, ob_prev, ssem.at[1 - p])

        _swiglu_acc(xb[...].astype(BF16), w1_ref, w3_ref, w2_ref, acc_ref, fc,
                    before_chunk=stage_weights if streaming else None, after_chunk=move_rows)

        @pl.when(i >= 1)
        def _():
            wait_scatter(ob, ssem.at[p])
        ob[...] = acc_ref[...]

    n_real = nreal_ref[0]

    @pl.when(is_first)
    def _():
        step(True)

    @pl.when(jnp.logical_and(i < n_real, jnp.logical_not(is_first)))
    def _():
        step(False)

    @pl.when(i == n_real)
    def _():
        wait_gather(xb, gsem.at[p])

        @pl.when(i >= 1)
        def _():
            wait_scatter(ob, ssem.at[p])
        lax.fori_loop(0, tm, lambda r, c: (scatter_row(i * tm, r, ob_prev, ssem.at[1 - p]), c)[1], 0)
        wait_scatter(ob_prev, ssem.at[1 - p])


def _moe(hm, route, w1, w3, w2, *, tm, fc):
    t, d = hm.shape
    n_exp = w1.shape[0]
    n_assign = t * TOP_K
    nt = n_assign // tm + n_exp
    assert n_assign % tm == 0
    e_flat = route[:, :TOP_K].astype(jnp.int32).reshape(-1)
    order = jnp.argsort(e_flat, stable=True).astype(jnp.int32)
    counts = jnp.sum((e_flat[:, None] == jnp.arange(n_exp, dtype=jnp.int32)[None, :]).astype(jnp.int32), axis=0)
    tiles_e = (counts + tm - 1) // tm
    tile_end = jnp.cumsum(tiles_e)
    tile_start = tile_end - tiles_e
    cnt_start = jnp.cumsum(counts) - counts
    tile_ids = jnp.arange(nt, dtype=jnp.int32)
    texp = jnp.minimum(jnp.sum((tile_ids[:, None] >= tile_end[None, :]).astype(jnp.int32), axis=1), n_exp - 1)
    within = (tile_ids - tile_start[texp]) * tm
    nval = jnp.where(tile_ids < tile_end[-1], jnp.clip(counts[texp] - within, 0, tm), 0).astype(jnp.int32)
    r_in = jnp.arange(tm, dtype=jnp.int32)[None, :]
    src = jnp.clip(cnt_start[texp][:, None] + within[:, None] + r_in, 0, n_assign - 1)
    real = r_in < nval[:, None]
    assign = jnp.where(real, order[src], 0)
    scratch_row = TOP_K * t + (tile_ids[:, None] % 2) * tm + r_in
    rtok = jnp.concatenate([(assign // TOP_K).reshape(-1), jnp.zeros((2 * tm,), jnp.int32)])
    rdst = jnp.concatenate([TOP_K * t + tm + r_in[0],
                            jnp.where(real, (assign % TOP_K) * t + assign // TOP_K, scratch_row).reshape(-1)])
    texp = jnp.concatenate([texp, texp[-1:], texp[-1:]])
    prev_exp = jnp.concatenate([jnp.full((1,), -1, jnp.int32), texp[:-1]])
    first = jnp.where(jnp.arange(nt + 2) < tile_end[-1], texp != prev_exp, False).astype(jnp.int32)
    n_steps = nt + 1
    d_ff = w1.shape[2]
    n_slots = 2
    assert d_ff % fc == 0 and d_ff // fc >= n_slots

    hbm = pl.BlockSpec(memory_space=pl.ANY)
    y2 = pl.pallas_call(
        functools.partial(_moe_kernel, tm=tm, fc=fc),
        grid_spec=pltpu.PrefetchScalarGridSpec(
            num_scalar_prefetch=5,
            grid=(n_steps,),
            in_specs=[hbm, hbm, hbm, hbm],
            out_specs=hbm,
            scratch_shapes=[pltpu.VMEM((2, tm, d), F32), pltpu.VMEM((2, tm, d), F32), pltpu.VMEM((tm, d), F32),
                            pltpu.VMEM((d, d_ff), BF16), pltpu.VMEM((d, d_ff), BF16), pltpu.VMEM((d_ff, d), BF16),
               pltpu.VMEM((n_slots, 2, d, fc), F32), pltpu.VMEM((n_slots, fc, d), F32),
               pltpu.SemaphoreType.DMA((2,)), pltpu.SemaphoreType.DMA((2,)), pltpu.SemaphoreType.DMA((n_slots,))],
        ),
        out_shape=jax.ShapeDtypeStruct((TOP_K * t + 2 * tm, d), F32),
        compiler_params=_params(1),
        name="moe",
    )(texp, first, tile_end[-1:].astype(jnp.int32), rtok, rdst, hm, w1, w3, w2)
    return y2


def _final_kernel(x4_ref, ya_ref, yb_ref, route_ref, pp_ref, ps_ref, gple_ref, wproj_ref, wgate_ref, gfin_ref,
                  yp_ref, ys_ref, *, n_prompt_tiles):
    i = pl.program_id(0)
    route = route_ref[...]
    g1 = route[:, 2:3]
    g2 = route[:, 3:4]
    x5 = x4_ref[...] + (g1 * ya_ref[...] + g2 * yb_ref[...])
    p = jnp.where(i < n_prompt_tiles, pp_ref[...], ps_ref[...]).astype(BF16)
    x6 = _ple_gate(x5, p, gple_ref[...], wproj_ref, wgate_ref)
    y = _rms(x6, gfin_ref[...])

    @pl.when(i < n_prompt_tiles)
    def _():
        yp_ref[...] = y

    @pl.when(i >= n_prompt_tiles)
    def _():
        ys_ref[...] = y


def _final(x4, y2, route, pp, ps, gple, wproj, wgate, gfin, *, tm):
    t, d = x4.shape
    tp, ts = pp.shape[1], ps.shape[1]
    npt = tp // tm
    assert tp % tm == 0 and ts % tm == 0
    row = lambda w: pl.BlockSpec((tm, w), lambda i: (i, 0))
    vec = pl.BlockSpec((1, d), lambda i: (0, 0))
    return pl.pallas_call(
        functools.partial(_final_kernel, n_prompt_tiles=npt),
        grid=(t // tm,),
        in_specs=[row(d), row(d), pl.BlockSpec((tm, d), lambda i: (t // tm + i, 0)), row(ROUTE_LANES)]
        + _split_specs(tm, pp.shape[2], npt, layer=1) + [vec, _const_spec(wproj.shape), _const_spec(wgate.shape), vec],
        out_specs=_split_specs(tm, d, npt),
        out_shape=[jax.ShapeDtypeStruct((tp, d), F32), jax.ShapeDtypeStruct((ts, d), F32)],
        compiler_params=_params(1),
        name="final",
    )(x4, y2, y2, route, pp, ps, gple, wproj, wgate, gfin)


def kernel(x_prompt, x_sample, state_pool, cache_k, cache_v, p_prompt, p_sample, g_mix, g_ffn, g_ple, g_kv, g_final,
           w_pool, pool_scale, w_q, b_q, w_k, b_k, w_v, b_v, w_o, sinks, w1_dense, w3_dense, w2_dense, w_router,
           b_router, w1_moe, w3_moe, w2_moe, w_ple_proj, w_ple_gate):
    bp, sp, d = x_prompt.shape
    bs, ss, _ = x_sample.shape
    depth = g_mix.shape[0]
    assert depth == 2 and w_pool.shape[0] == 1 and w_q.shape[0] == 1 and w1_dense.shape[0] == 1 and w1_moe.shape[0] == 1
    tp, ts = bp * sp, bs * ss
    n_kv, hd = cache_k.shape[2], cache_k.shape[3]
    n_exp = w_router.shape[2]
    ple = p_prompt.shape[-1]
    bf = lambda w: w.astype(BF16)
    vec = lambda g: g.reshape(1, -1)
    pp = p_prompt.reshape(depth, tp, ple)
    ps = p_sample.reshape(depth, ts, ple)

    x1_p, x1_s, hf_p, hf_s, pool_p, pool_s = _pool_front(
        x_prompt, x_sample, state_pool[0], vec(g_mix[0]), vec(g_ffn[0]), bf(w_pool[0]), vec(pool_scale[0]),
        ta=512, nbs=8)

    back = cache_k.shape[1]
    ck = cache_k.reshape(bs, back, n_kv * hd)
    cv = cache_v.reshape(bs, back, n_kv * hd)
    x3, k, v, q, kc_p, vc_p = _ffn0_post(
        hf_p, hf_s, x1_p, x1_s, pp, ps,
        bf(w1_dense[0]), bf(w3_dense[0]), bf(w2_dense[0]),
        vec(g_ple[0]), bf(w_ple_proj[0]), bf(w_ple_gate[0]),
        vec(g_kv), bf(w_k), vec(b_k), bf(w_v), vec(b_v),
        vec(g_mix[1]), bf(w_q[0]), vec(b_q[0]), tm=512, fc=512, sp=sp, back=back)

    o_p, o_s, kc_s, vc_s = _swa(q, k, v, ck, cv, sinks[0], bp=bp, sp=sp, bs=bs, ss=ss, tq=N_BACK * CHUNK)

    wr = jnp.pad(w_router[0], ((0, 0), (0, ROUTE_LANES - n_exp)))
    br = jnp.pad(b_router[0], (0, ROUTE_LANES - n_exp), constant_values=NEG).reshape(1, ROUTE_LANES)
    x4, hm, route = _oproj_route(o_p, o_s, x3, bf(w_o[0]), vec(g_ffn[1]), wr, br, tm=512)

    y2 = _moe(hm, route, w1_moe[0], w3_moe[0], w2_moe[0], tm=512, fc=512)

    y_p, y_s = _final(x4, y2, route, pp, ps,
                      vec(g_ple[1]), bf(w_ple_proj[1]), bf(w_ple_gate[1]), vec(g_final), tm=512)

    heads = lambda c: c.reshape(c.shape[0], back, n_kv, hd)
    return (y_p.reshape(bp, sp, d), y_s.reshape(bs, ss, d), pool_p, pool_s,
            heads(kc_p), heads(vc_p), heads(kc_s), heads(vc_s))
```

```python
import functools

import jax
import jax.numpy as jnp
from jax import lax
from jax.experimental import pallas as pl
from jax.experimental.pallas import tpu as pltpu

EPS = 1e-6
NEG = -1e30
CHUNK = 64
N_BACK = 2
POOL_WINDOWS = (2, 4, 8, 16)
POOL_BUF = 15
HIST = 16
TOP_K = 2
ROUTE_LANES = 128
VMEM_LIMIT = 56 * 1024 * 1024

F32 = jnp.float32
BF16 = jnp.bfloat16


def _rms(x, g):
    return (x * lax.rsqrt(jnp.mean(x * x, axis=-1, keepdims=True) + EPS)) * g


def _dot(a, b):
    return jnp.dot(a, b, preferred_element_type=F32)


def _const_spec(shape):
    nd = len(shape)
    return pl.BlockSpec(shape, lambda *_: (0,) * nd, pipeline_mode=pl.Buffered(1))


def _params(n_axes):
    return pltpu.CompilerParams(dimension_semantics=("arbitrary",) * n_axes, vmem_limit_bytes=VMEM_LIMIT)


def _pool_front_kernel(*refs, prompt, nb, ta, d):
    (x_ref, hist_ref, gmix_ref, gffn_ref, wpool_ref, scale_ref,
     x1_ref, hf_ref, state_ref, e_ref, f2_ref, f4_ref, f8_ref) = refs
    gw = d // len(POOL_WINDOWS)
    j = pl.program_id(1)
    n = ta + HIST
    x = x_ref[...]
    g = gmix_ref[...]
    h = _rms(x, g)
    if prompt:
        hp = _rms(hist_ref[...], g)
        hp = jnp.where(j == 0, 0.0, hp)
        e_ref[:, 0:HIST, :] = hp
    else:
        e_ref[:, 0:1, :] = jnp.zeros((nb, 1, d), F32)
        e_ref[:, 1:HIST, :] = hist_ref[...]
    e_ref[:, HIST:n, :] = h
    zero_tail = jnp.zeros((nb, 8, d), F32)
    e_ref[:, n:n + 8, :] = zero_tail
    f2_ref[:, 0:n, :] = e_ref[:, 0:n, :] + e_ref[:, 1:n + 1, :]
    f2_ref[:, n:n + 8, :] = zero_tail
    f4_ref[:, 0:n, :] = f2_ref[:, 0:n, gw:] + f2_ref[:, 2:n + 2, gw:]
    f4_ref[:, n:n + 8, :] = zero_tail[:, :, gw:]
    f8_ref[:, 0:n, :] = f4_ref[:, 0:n, gw:] + f4_ref[:, 4:n + 4, gw:]
    f8_ref[:, n:n + 8, :] = zero_tail[:, :, 2 * gw:]
    sums = (
        f2_ref[:, HIST - 1:HIST - 1 + ta, 0:gw],
        f4_ref[:, HIST - 3:HIST - 3 + ta, 0:gw],
        f8_ref[:, HIST - 7:HIST - 7 + ta, 0:gw],
        f8_ref[:, HIST - 15:HIST - 15 + ta, gw:] + f8_ref[:, HIST - 7:HIST - 7 + ta, gw:],
    )
    t_idx = lax.broadcasted_iota(jnp.int32, (1, ta, 1), 1) + j * ta
    ys = []
    for gi, w in enumerate(POOL_WINDOWS):
        if prompt:
            cnt = jnp.minimum(t_idx + 1, w).astype(F32)
            mean = sums[gi] / cnt
        else:
            mean = sums[gi] * (1.0 / w)
        diff = mean - h[:, :, gi * gw:(gi + 1) * gw]
        ys.append(_dot(diff.reshape(nb * ta, gw).astype(BF16), wpool_ref[gi]))
    y = jnp.concatenate(ys, axis=-1) * scale_ref[...]
    x1 = x.reshape(nb * ta, d) + y
    x1_ref[...] = x1
    hf_ref[...] = _rms(x1, gffn_ref[...]).astype(BF16)
    if prompt:
        @pl.when(j == pl.num_programs(1) - 1)
        def _():
            state_ref[0] = e_ref[:, n - POOL_BUF:n, :]
    else:
        state_ref[0] = e_ref[:, n - POOL_BUF:n, :]


def _pool_front(x_p, x_s, state, gmix, gffn, wpool, scale, *, ta, nbs):
    bp, sp, d = x_p.shape
    bs, ss, _ = x_s.shape
    tp, ts = bp * sp, bs * ss
    gw = d // len(POOL_WINDOWS)
    assert sp % ta == 0 and ta % HIST == 0 and bs % nbs == 0
    vec = pl.BlockSpec((1, d), lambda *_: (0, 0))
    wspec = pl.BlockSpec((len(POOL_WINDOWS), gw, gw), lambda *_: (0, 0, 0))

    def scratch(nb, rows):
        return [pltpu.VMEM((nb, rows + HIST + 8, d), F32), pltpu.VMEM((nb, rows + HIST + 8, d), F32),
                pltpu.VMEM((nb, rows + HIST + 8, d - gw), F32), pltpu.VMEM((nb, rows + HIST + 8, d - 2 * gw), F32)]

    njp = sp // ta
    x1_p, hf_p, state_p = pl.pallas_call(
        functools.partial(_pool_front_kernel, prompt=True, nb=1, ta=ta, d=d),
        grid=(bp, njp),
        in_specs=[
            pl.BlockSpec((1, ta, d), lambda b, j: (b, j, 0)),
            pl.BlockSpec((1, HIST, d), lambda b, j: (b, jnp.maximum(j * (ta // HIST) - 1, 0), 0)),
            vec, vec, wspec, vec,
        ],
        out_specs=[
            pl.BlockSpec((ta, d), lambda b, j: (b * njp + j, 0)),
            pl.BlockSpec((ta, d), lambda b, j: (b * njp + j, 0)),
            pl.BlockSpec((1, 1, POOL_BUF, d), lambda b, j: (0, b, 0, 0)),
        ],
        out_shape=[jax.ShapeDtypeStruct((tp, d), F32), jax.ShapeDtypeStruct((tp, d), BF16),
                   jax.ShapeDtypeStruct((1, bp, POOL_BUF, d), F32)],
        scratch_shapes=scratch(1, ta),
        compiler_params=_params(2),
        name="pool_front_prompt",
    )(x_p, x_p, gmix, gffn, wpool, scale)

    rows = nbs * ss
    x1_s, hf_s, state_s = pl.pallas_call(
        functools.partial(_pool_front_kernel, prompt=False, nb=nbs, ta=ss, d=d),
        grid=(bs // nbs, 1),
        in_specs=[
            pl.BlockSpec((nbs, ss, d), lambda b, j: (b, 0, 0)),
            pl.BlockSpec((nbs, POOL_BUF, d), lambda b, j: (b, 0, 0)),
            vec, vec, wspec, vec,
        ],
        out_specs=[
            pl.BlockSpec((rows, d), lambda b, j: (b, 0)),
            pl.BlockSpec((rows, d), lambda b, j: (b, 0)),
            pl.BlockSpec((1, nbs, POOL_BUF, d), lambda b, j: (0, b, 0, 0)),
        ],
        out_shape=[jax.ShapeDtypeStruct((ts, d), F32), jax.ShapeDtypeStruct((ts, d), BF16),
                   jax.ShapeDtypeStruct((1, bs, POOL_BUF, d), F32)],
        scratch_shapes=scratch(nbs, ss),
        compiler_params=_params(2),
        name="pool_front_sample",
    )(x_s, state, gmix, gffn, wpool, scale)
    return x1_p, x1_s, hf_p, hf_s, state_p, state_s


def _swiglu_acc(x_bf, w1_ref, w3_ref, w2_ref, acc_ref, fc, before_chunk=None, after_chunk=None):
    d_ff = w1_ref.shape[-1]
    n_chunks = d_ff // fc
    for c in range(n_chunks):
        if before_chunk is not None:
            before_chunk(c, n_chunks)
        sl = slice(c * fc, (c + 1) * fc)
        h1 = _dot(x_bf, w1_ref[:, sl])
        h3 = _dot(x_bf, w3_ref[:, sl])
        a = (h1 * jax.nn.sigmoid(h1)) * h3
        acc_ref[...] += _dot(a.astype(BF16), w2_ref[sl, :])
        if after_chunk is not None:
            after_chunk(c, n_chunks)


def _ple_gate(x, p_bf, gple, wproj_ref, wgate_ref):
    e = _dot(p_bf, wproj_ref[...])
    gate = jax.nn.sigmoid(_dot(_rms(x, gple).astype(BF16), wgate_ref[...]))
    return x + gate * e


def _ffn0_post_kernel(hfp_ref, hfs_ref, x1p_ref, x1s_ref, pp_ref, ps_ref, w1_ref, w3_ref, w2_ref,
                      gple_ref, wproj_ref, wgate_ref, gkv_ref, wk_ref, bk_ref, wv_ref, bv_ref,
                      gmix_ref, wq_ref, bq_ref,
                      x3_ref, k_ref, v_ref, q_ref, kcp_ref, vcp_ref, acc_ref,
                      *, fc, n_prompt_tiles, tiles_per_seq):
    i = pl.program_id(0)
    is_prompt = i < n_prompt_tiles
    acc_ref[...] = jnp.where(is_prompt, x1p_ref[...], x1s_ref[...])
    _swiglu_acc(jnp.where(is_prompt, hfp_ref[...], hfs_ref[...]), w1_ref, w3_ref, w2_ref, acc_ref, fc)
    p = jnp.where(is_prompt, pp_ref[...], ps_ref[...]).astype(BF16)
    x3 = _ple_gate(acc_ref[...], p, gple_ref[...], wproj_ref, wgate_ref)
    x3_ref[...] = x3
    kv_in = _rms(x3, gkv_ref[...]).astype(BF16)
    k = _dot(kv_in, wk_ref[...]) + bk_ref[...]
    v = _dot(kv_in, wv_ref[...]) + bv_ref[...]
    k_ref[...] = k
    v_ref[...] = v
    hq = _rms(x3, gmix_ref[...]).astype(BF16)
    q_ref[...] = (_dot(hq, wq_ref[...]) + bq_ref[...]).astype(BF16)

    back = kcp_ref.shape[1]

    @pl.when(jnp.logical_and(is_prompt, i % tiles_per_seq == tiles_per_seq - 1))
    def _():
        kcp_ref[0] = k[k.shape[0] - back:]
        vcp_ref[0] = v[v.shape[0] - back:]


def _split_specs(tm, width, n_prompt_tiles, layer=None, lean=False):
    kw = dict(pipeline_mode=pl.Buffered(1)) if lean else {}
    if layer is None:
        return [pl.BlockSpec((tm, width), lambda i, *_: (jnp.minimum(i, n_prompt_tiles - 1), 0)),
                pl.BlockSpec((tm, width), lambda i, *_: (jnp.maximum(i - n_prompt_tiles, 0), 0), **kw)]
    return [pl.BlockSpec((None, tm, width), lambda i, *_: (layer, jnp.minimum(i, n_prompt_tiles - 1), 0)),
            pl.BlockSpec((None, tm, width), lambda i, *_: (layer, jnp.maximum(i - n_prompt_tiles, 0), 0), **kw)]


def _ffn0_post(hf_p, hf_s, x1_p, x1_s, pp, ps, w1, w3, w2, gple, wproj, wgate, gkv, wk, bk, wv, bv,
               gmix, wq, bq, *, tm, fc, sp, back):
    tp, d = x1_p.shape
    ts = x1_s.shape[0]
    t = tp + ts
    kvw = wk.shape[1]
    bp = tp // sp
    assert ts % tm == 0 and tp % tm == 0 and w1.shape[1] % fc == 0 and sp % tm == 0 and back <= tm
    npt = tp // tm
    row = lambda w: pl.BlockSpec((tm, w), lambda i: (i, 0))
    vec = lambda w: pl.BlockSpec((1, w), lambda i: (0, 0))
    pcache = pl.BlockSpec((1, back, kvw), lambda i: (jnp.minimum(i // (sp // tm), bp - 1), 0, 0))
    cache_shape = jax.ShapeDtypeStruct((bp, back, kvw), F32)
    return pl.pallas_call(
        functools.partial(_ffn0_post_kernel, fc=fc, n_prompt_tiles=npt, tiles_per_seq=sp // tm),
        grid=(t // tm,),
        in_specs=_split_specs(tm, d, npt) + _split_specs(tm, d, npt) + _split_specs(tm, pp.shape[2], npt, layer=0) + [
            _const_spec(w1.shape), _const_spec(w3.shape), _const_spec(w2.shape),
            vec(d), _const_spec(wproj.shape), _const_spec(wgate.shape),
            vec(d), _const_spec(wk.shape), vec(kvw), _const_spec(wv.shape), vec(kvw),
            vec(d), _const_spec(wq.shape), vec(d),
        ],
        out_specs=[row(d), row(kvw), row(kvw), row(d), pcache, pcache],
        out_shape=[jax.ShapeDtypeStruct((t, d), F32), jax.ShapeDtypeStruct((t, kvw), F32),
                   jax.ShapeDtypeStruct((t, kvw), F32), jax.ShapeDtypeStruct((t, d), BF16), cache_shape, cache_shape],
        scratch_shapes=[pltpu.VMEM((tm, d), F32)],
        compiler_params=_params(1),
        name="ffn0_post",
    )(hf_p, hf_s, x1_p, x1_s, pp, ps, w1, w3, w2, gple, wproj, wgate, gkv, wk, bk, wv, bv, gmix, wq, bq)


def _swa_kernel(*refs, prompt, tq, n_kv, group, hd):
    lanes = 128
    per = lanes // hd
    assert per == 2 and group % per == 0
    if prompt:
        n_sub = (len(refs) - 3) // 2 - 1
        sink_ref, q_ref, o_ref = refs[0], refs[1], refs[-1]
        kblk = [r[...] for r in refs[2:3 + n_sub]]
        vblk = [r[...] for r in refs[3 + n_sub:4 + 2 * n_sub]]
        tiles = [(s * tq, jnp.concatenate(kblk[s:s + 2], axis=0), jnp.concatenate(vblk[s:s + 2], axis=0),
                  N_BACK - (n_sub * pl.program_id(1) + s) * (tq // CHUNK)) for s in range(n_sub)]
    else:
        sink_ref, q_ref, ck_ref, kn_ref, cv_ref, vn_ref, o_ref, kc_ref, vc_ref = refs
        pad = jnp.zeros((tq, kn_ref.shape[1]), F32)
        back = ck_ref.shape[1]
        tiles = []
        for s in range(ck_ref.shape[0]):
            kk = jnp.concatenate([ck_ref[s], kn_ref[s * tq:(s + 1) * tq], pad], axis=0)
            vv = jnp.concatenate([cv_ref[s], vn_ref[s * tq:(s + 1) * tq], pad], axis=0)
            kc_ref[s] = kk[tq:tq + back]
            vc_ref[s] = vv[tq:tq + back]
            tiles.append((s * tq, kk, vv, 0))
    for r0, kk, vv, first_chunk in tiles:
        _swa_tile(sink_ref, q_ref, o_ref, r0, kk, vv, first_chunk, tq=tq, n_kv=n_kv, group=group, hd=hd)


def _swa_tile(sink_ref, q_ref, o_ref, r0, kk, vv, first_chunk, *, tq, n_kv, group, hd):
    nk = kk.shape[0]
    lanes = 128
    per = lanes // hd
    assert nk % lanes == 0
    rows = slice(r0, r0 + tq)
    qc = lax.broadcasted_iota(jnp.int32, (tq, per * nk), 0) // CHUNK
    kc = (lax.broadcasted_iota(jnp.int32, (tq, per * nk), 1) % nk) // CHUNK
    bias = jnp.where(kc >= jnp.maximum(qc, first_chunk), jnp.where(kc <= qc + N_BACK, 0.0, NEG), NEG)
    lane = lax.broadcasted_iota(jnp.int32, (nk, lanes), 1)
    ones_a = jnp.where(lane < hd, 1.0, 0.0)
    ones_b = jnp.where(lane >= hd, 1.0, 0.0)
    out_lane = lax.broadcasted_iota(jnp.int32, (tq, lanes), 1)
    for kvh in range(n_kv):
        tile, half = divmod(kvh, per)
        keep = (lane < hd) if half == 0 else (lane >= hd)

        def both_halves(x):
            own = jnp.where(keep, x[:, tile * lanes:(tile + 1) * lanes], 0.0)
            other = pltpu.roll(own, hd, axis=1)
            return (own, other) if half == 0 else (other, own)

        k_a, k_b = both_halves(kk * (hd ** -0.5))
        v_a, v_b = both_halves(vv)
        kbd = jnp.concatenate([k_a, k_b], axis=0).astype(BF16)
        vobd = jnp.concatenate([jnp.concatenate([v_a, ones_a], axis=1),
                                jnp.concatenate([v_b, ones_b], axis=1)], axis=0).astype(BF16)
        for pr in range(group // per):
            head_a = kvh * group + pr * per
            col0 = head_a * hd
            s = lax.dot_general(q_ref[rows, col0:col0 + lanes], kbd, (((1,), (1,)), ((), ())),
                                preferred_element_type=F32) + bias
            sink_a, sink_b = sink_ref[head_a], sink_ref[head_a + 1]
            m_a = jnp.maximum(jnp.max(s[:, :nk], axis=-1, keepdims=True), sink_a)
            m_b = jnp.maximum(jnp.max(s[:, nk:], axis=-1, keepdims=True), sink_b)
            p = jnp.concatenate([jnp.exp(s[:, :nk] - m_a), jnp.exp(s[:, nk:] - m_b)], axis=1).astype(BF16)
            r = _dot(p, vobd)
            sink_term = jnp.where(out_lane < hd, jnp.exp(sink_a - m_a), jnp.exp(sink_b - m_b))
            o_ref[rows, col0:col0 + lanes] = (r[:, :lanes] / (r[:, lanes:] + sink_term)).astype(BF16)


def _swa(q, k, v, cache_k, cache_v, sinks, *, bp, sp, bs, ss, tq, n_sub):
    t, d = q.shape
    kvw = k.shape[1]
    tp = bp * sp
    n_heads = sinks.shape[0]
    hd = d // n_heads
    n_kv = kvw // hd
    back = N_BACK * CHUNK
    assert tq == back and sp % tq == 0 and ss == CHUNK and cache_k.shape[1] == back
    smem = pl.BlockSpec(memory_space=pltpu.SMEM)
    nj = sp // tq
    kw = dict(n_kv=n_kv, group=n_heads // n_kv, hd=hd)
    assert nj % n_sub == 0
    njs = nj // n_sub
    kv_blk = lambda s: pl.BlockSpec((tq, kvw), lambda b, j: (b * nj + jnp.maximum(n_sub * j + s, 0), 0))
    kv_blks = [kv_blk(s) for s in range(-1, n_sub)]
    o_p = pl.pallas_call(
        functools.partial(_swa_kernel, prompt=True, tq=tq, **kw),
        grid=(bp, njs),
        in_specs=[smem, pl.BlockSpec((n_sub * tq, d), lambda b, j: (b * njs + j, 0))] + kv_blks + kv_blks,
        out_specs=pl.BlockSpec((n_sub * tq, d), lambda b, j: (b * njs + j, 0)),
        out_shape=jax.ShapeDtypeStruct((tp, d), BF16),
        compiler_params=_params(2),
        name="swa_prompt",
    )(sinks, q, *([k] * (n_sub + 1)), *([v] * (n_sub + 1)))
    rows = n_sub * ss
    assert bs % n_sub == 0 and tp % rows == 0
    off = tp // rows
    cache_blk = pl.BlockSpec((n_sub, back, kvw), lambda b: (b, 0, 0))
    new_blk = pl.BlockSpec((rows, kvw), lambda b: (off + b, 0))
    o_s, kc_s, vc_s = pl.pallas_call(
        functools.partial(_swa_kernel, prompt=False, tq=ss, **kw),
        grid=(bs // n_sub,),
        in_specs=[smem, pl.BlockSpec((rows, d), lambda b: (off + b, 0)), cache_blk, new_blk, cache_blk, new_blk],
        out_specs=[pl.BlockSpec((rows, d), lambda b: (b, 0)), cache_blk, cache_blk],
        out_shape=[jax.ShapeDtypeStruct((bs * ss, d), BF16), jax.ShapeDtypeStruct((bs, back, kvw), F32),
                   jax.ShapeDtypeStruct((bs, back, kvw), F32)],
        compiler_params=_params(1),
        name="swa_sample",
    )(sinks, q, cache_k, k, cache_v, v)
    return o_p, o_s, kc_s, vc_s


def _oproj_route_kernel(op_ref, os_ref, x3_ref, wo_ref, gffn_ref, wr_ref, br_ref, x4_ref, hm_ref, route_ref,
                        *, n_prompt_tiles):
    o = jnp.where(pl.program_id(0) < n_prompt_tiles, op_ref[...], os_ref[...])
    x4 = x3_ref[...] + _dot(o, wo_ref[...])
    x4_ref[...] = x4
    hm = _rms(x4, gffn_ref[...])
    hm_ref[...] = hm
    hm_hi = hm.astype(BF16)
    hm_lo = (hm - hm_hi.astype(F32)).astype(BF16)
    wr = wr_ref[...]
    wr_hi = wr.astype(BF16)
    wr_lo = (wr - wr_hi.astype(F32)).astype(BF16)
    hi_both = _dot(hm_hi, jnp.concatenate([wr_hi, wr_lo], axis=1))
    logits = (hi_both[:, :ROUTE_LANES] + (_dot(hm_lo, wr_hi) + hi_both[:, ROUTE_LANES:])) + br_ref[...]
    lane = lax.broadcasted_iota(jnp.int32, logits.shape, 1).astype(F32)
    m1 = jnp.max(logits, axis=-1, keepdims=True)
    i1 = jnp.min(jnp.where(logits == m1, lane, float(ROUTE_LANES)), axis=-1, keepdims=True)
    rest = jnp.where(lane == i1, -jnp.inf, logits)
    m2 = jnp.max(rest, axis=-1, keepdims=True)
    i2 = jnp.min(jnp.where(rest == m2, lane, float(ROUTE_LANES)), axis=-1, keepdims=True)
    e2 = jnp.exp(m2 - m1)
    g1 = 1.0 / (1.0 + e2)
    g2 = e2 / (1.0 + e2)
    route = jnp.where(lane == 0.0, i1, jnp.where(lane == 1.0, i2,
                      jnp.where(lane == 2.0, g1, jnp.where(lane == 3.0, g2, 0.0))))
    route_ref[...] = route


def _oproj_route(o_p, o_s, x3, wo, gffn, wr, br, *, tm):
    t, d = x3.shape
    tp = o_p.shape[0]
    assert tp % tm == 0 and o_s.shape[0] % tm == 0
    row = lambda w: pl.BlockSpec((tm, w), lambda i: (i, 0))
    return pl.pallas_call(
        functools.partial(_oproj_route_kernel, n_prompt_tiles=tp // tm),
        grid=(t // tm,),
        in_specs=_split_specs(tm, d, tp // tm) + [row(d), _const_spec(wo.shape), pl.BlockSpec((1, d), lambda i: (0, 0)),
                  _const_spec(wr.shape), pl.BlockSpec((1, ROUTE_LANES), lambda i: (0, 0))],
        out_specs=[row(d), row(d), row(ROUTE_LANES)],
        out_shape=[jax.ShapeDtypeStruct((t, d), F32), jax.ShapeDtypeStruct((t, d), F32),
                   jax.ShapeDtypeStruct((t, ROUTE_LANES), F32)],
        compiler_params=_params(1),
        name="oproj_route",
    )(o_p, o_s, x3, wo, gffn, wr, br)


def _moe_kernel(texp_ref, first_ref, nreal_ref, rtok_ref, rdst_ref, hm_ref, w1_hbm, w3_hbm, w2_hbm, y_ref,
                xbuf, obuf, acc_ref, w1_ref, w3_ref, w2_ref, st13, st2, gsem, ssem, wsem,
                *, tm, fc):
    i = pl.program_id(0)
    n_slots = st2.shape[0]

    def gather_row(base, r, xb, sem):
        tok = rtok_ref[base + r]
        pltpu.make_async_copy(hm_ref.at[pl.ds(tok, 1), :], xb.at[pl.ds(r, 1), :], sem).start()

    def scatter_row(base, r, ob, sem):
        dst = rdst_ref[base + r]
        pltpu.make_async_copy(ob.at[pl.ds(r, 1), :], y_ref.at[pl.ds(dst, 1), :], sem).start()

    def weight_copies(c, step=None):
        e = texp_ref[i if step is None else step]
        slot = c % n_slots
        cols = pl.ds(c * fc, fc)
        return (pltpu.make_async_copy(w1_hbm.at[e, :, cols], st13.at[slot, 0], wsem.at[slot]),
                pltpu.make_async_copy(w3_hbm.at[e, :, cols], st13.at[slot, 1], wsem.at[slot]),
                pltpu.make_async_copy(w2_hbm.at[e, cols, :], st2.at[slot], wsem.at[slot]))

    def start_weights(c, step=None):
        for cp in weight_copies(c, step):
            cp.start(1)

    def stage_weights(c, n_chunks):
        for cp in weight_copies(c):
            cp.wait()
        slot = c % n_slots
        w1_ref[:, c * fc:(c + 1) * fc] = st13[slot, 0].astype(BF16)
        w3_ref[:, c * fc:(c + 1) * fc] = st13[slot, 1].astype(BF16)
        w2_ref[c * fc:(c + 1) * fc, :] = st2[slot].astype(BF16)
        if c + n_slots < n_chunks:
            start_weights(c + n_slots)

    def wait_gather(xb, sem):
        pltpu.make_async_copy(hm_ref.at[pl.ds(0, tm), :], xb, sem).wait()

    def wait_scatter(ob, sem):
        pltpu.make_async_copy(ob, y_ref.at[pl.ds(0, tm), :], sem).wait()

    @pl.when(i == 0)
    def _():
        obuf[...] = jnp.zeros(obuf.shape, F32)
        lax.fori_loop(0, tm, lambda r, c: (gather_row(0, r, xbuf.at[0], gsem.at[0]), c)[1], 0)
        n_out = y_ref.shape[0] - 2 * tm
        fills = [pltpu.make_async_copy(obuf.at[s], y_ref.at[pl.ds(n_out + s * tm, tm), :], ssem.at[s])
                 for s in range(2)]
        for cp in fills:
            cp.start()
        for cp in fills:
            cp.wait()

    is_first = first_ref[i] == 1
    prev_first = jnp.logical_or(i == 0, first_ref[jnp.maximum(i - 1, 0)] == 1)

    @pl.when(jnp.logical_and(is_first, prev_first))
    def _():
        for c in range(n_slots):
            start_weights(c)

    @pl.when(jnp.logical_and(first_ref[i + 1] == 1, jnp.logical_not(is_first)))
    def _():
        for c in range(n_slots):
            start_weights(c, i + 1)

    p = i % 2
    xb, ob, xb_next, ob_prev = xbuf.at[p], obuf.at[p], xbuf.at[1 - p], obuf.at[1 - p]

    def step(streaming):
        wait_gather(xb, gsem.at[p])
        acc_ref[...] = jnp.zeros(acc_ref.shape, F32)
        g_base = (i + 1) * tm
        s_base = i * tm

        def move_rows(c, n_chunks):
            per = -(-2 * tm // n_chunks)
            for m in range(c * per, min((c + 1) * per, 2 * tm)):
                if m < tm:
                    gather_row(g_base, m, xb_next, gsem.at[1 - p])
                else:
                    scatter_row(s_base, m - tm, ob_prev, ssem.at[1 - p])

        _swiglu_acc(xb[...].astype(BF16), w1_ref, w3_ref, w2_ref, acc_ref, fc,
                    before_chunk=stage_weights if streaming else None, after_chunk=move_rows)

        @pl.when(i >= 1)
        def _():
            wait_scatter(ob, ssem.at[p])
        ob[...] = acc_ref[...]

    n_real = nreal_ref[0]

    @pl.when(is_first)
    def _():
        step(True)

    @pl.when(jnp.logical_and(i < n_real, jnp.logical_not(is_first)))
    def _():
        step(False)

    @pl.when(i == n_real)
    def _():
        wait_gather(xb, gsem.at[p])

        @pl.when(i >= 1)
        def _():
            wait_scatter(ob, ssem.at[p])
        lax.fori_loop(0, tm, lambda r, c: (scatter_row(i * tm, r, ob_prev, ssem.at[1 - p]), c)[1], 0)
        wait_scatter(ob_prev, ssem.at[1 - p])


def _moe(hm, route, w1, w3, w2, *, tm, fc):
    t, d = hm.shape
    n_exp = w1.shape[0]
    n_assign = t * TOP_K
    nt = n_assign // tm + n_exp
    assert n_assign % tm == 0
    e_flat = route[:, :TOP_K].astype(jnp.int32).reshape(-1)
    order = jnp.argsort(e_flat, stable=True).astype(jnp.int32)
    counts = jnp.sum((e_flat[:, None] == jnp.arange(n_exp, dtype=jnp.int32)[None, :]).astype(jnp.int32), axis=0)
    tiles_e = (counts + tm - 1) // tm
    tile_end = jnp.cumsum(tiles_e)
    tile_start = tile_end - tiles_e
    cnt_start = jnp.cumsum(counts) - counts
    tile_ids = jnp.arange(nt, dtype=jnp.int32)
    texp = jnp.minimum(jnp.sum((tile_ids[:, None] >= tile_end[None, :]).astype(jnp.int32), axis=1), n_exp - 1)
    within = (tile_ids - tile_start[texp]) * tm
    nval = jnp.where(tile_ids < tile_end[-1], jnp.clip(counts[texp] - within, 0, tm), 0).astype(jnp.int32)
    r_in = jnp.arange(tm, dtype=jnp.int32)[None, :]
    src = jnp.clip(cnt_start[texp][:, None] + within[:, None] + r_in, 0, n_assign - 1)
    real = r_in < nval[:, None]
    assign = jnp.where(real, order[src], 0)
    scratch_row = TOP_K * t + (tile_ids[:, None] % 2) * tm + r_in
    rtok = jnp.concatenate([(assign // TOP_K).reshape(-1), jnp.zeros((2 * tm,), jnp.int32)])
    rdst = jnp.concatenate([TOP_K * t + tm + r_in[0],
                            jnp.where(real, (assign % TOP_K) * t + assign // TOP_K, scratch_row).reshape(-1)])
    texp = jnp.concatenate([texp, texp[-1:], texp[-1:]])
    prev_exp = jnp.concatenate([jnp.full((1,), -1, jnp.int32), texp[:-1]])
    first = jnp.where(jnp.arange(nt + 2) < tile_end[-1], texp != prev_exp, False).astype(jnp.int32)
    n_steps = nt + 1
    d_ff = w1.shape[2]
    n_slots = 3
    assert d_ff % fc == 0 and d_ff // fc >= n_slots

    hbm = pl.BlockSpec(memory_space=pl.ANY)
    y2 = pl.pallas_call(
        functools.partial(_moe_kernel, tm=tm, fc=fc),
        grid_spec=pltpu.PrefetchScalarGridSpec(
            num_scalar_prefetch=5,
            grid=(n_steps,),
            in_specs=[hbm, hbm, hbm, hbm],
            out_specs=hbm,
            scratch_shapes=[pltpu.VMEM((2, tm, d), F32), pltpu.VMEM((2, tm, d), F32), pltpu.VMEM((tm, d), F32),
                            pltpu.VMEM((d, d_ff), BF16), pltpu.VMEM((d, d_ff), BF16), pltpu.VMEM((d_ff, d), BF16),
               pltpu.VMEM((n_slots, 2, d, fc), F32), pltpu.VMEM((n_slots, fc, d), F32),
               pltpu.SemaphoreType.DMA((2,)), pltpu.SemaphoreType.DMA((2,)), pltpu.SemaphoreType.DMA((n_slots,))],
        ),
        out_shape=jax.ShapeDtypeStruct((TOP_K * t + 2 * tm, d), F32),
        compiler_params=_params(1),
        name="moe",
    )(texp, first, tile_end[-1:].astype(jnp.int32), rtok, rdst, hm, w1, w3, w2)
    return y2


def _final_kernel(x4_ref, ya_ref, yb_ref, route_ref, pp_ref, ps_ref, gple_ref, wproj_ref, wgate_ref, gfin_ref,
                  yp_ref, ys_ref, *, n_prompt_tiles):
    i = pl.program_id(0)
    route = route_ref[...]
    g1 = route[:, 2:3]
    g2 = route[:, 3:4]
    x5 = x4_ref[...] + (g1 * ya_ref[...] + g2 * yb_ref[...])
    p = jnp.where(i < n_prompt_tiles, pp_ref[...], ps_ref[...]).astype(BF16)
    x6 = _ple_gate(x5, p, gple_ref[...], wproj_ref, wgate_ref)
    y = _rms(x6, gfin_ref[...])

    @pl.when(i < n_prompt_tiles)
    def _():
        yp_ref[...] = y

    @pl.when(i >= n_prompt_tiles)
    def _():
        ys_ref[...] = y


def _final(x4, y2, route, pp, ps, gple, wproj, wgate, gfin, *, tm):
    t, d = x4.shape
    tp, ts = pp.shape[1], ps.shape[1]
    npt = tp // tm
    assert tp % tm == 0 and ts % tm == 0
    row = lambda w: pl.BlockSpec((tm, w), lambda i: (i, 0))
    vec = pl.BlockSpec((1, d), lambda i: (0, 0))
    return pl.pallas_call(
        functools.partial(_final_kernel, n_prompt_tiles=npt),
        grid=(t // tm,),
        in_specs=[row(d), row(d), pl.BlockSpec((tm, d), lambda i: (t // tm + i, 0)), row(ROUTE_LANES)]
        + _split_specs(tm, pp.shape[2], npt, layer=1) + [vec, _const_spec(wproj.shape), _const_spec(wgate.shape), vec],
        out_specs=_split_specs(tm, d, npt),
        out_shape=[jax.ShapeDtypeStruct((tp, d), F32), jax.ShapeDtypeStruct((ts, d), F32)],
        compiler_params=_params(1),
        name="final",
    )(x4, y2, y2, route, pp, ps, gple, wproj, wgate, gfin)


def kernel(x_prompt, x_sample, state_pool, cache_k, cache_v, p_prompt, p_sample, g_mix, g_ffn, g_ple, g_kv, g_final,
           w_pool, pool_scale, w_q, b_q, w_k, b_k, w_v, b_v, w_o, sinks, w1_dense, w3_dense, w2_dense, w_router,
           b_router, w1_moe, w3_moe, w2_moe, w_ple_proj, w_ple_gate):
    bp, sp, d = x_prompt.shape
    bs, ss, _ = x_sample.shape
    depth = g_mix.shape[0]
    assert depth == 2 and w_pool.shape[0] == 1 and w_q.shape[0] == 1 and w1_dense.shape[0] == 1 and w1_moe.shape[0] == 1
    tp, ts = bp * sp, bs * ss
    n_kv, hd = cache_k.shape[2], cache_k.shape[3]
    n_exp = w_router.shape[2]
    ple = p_prompt.shape[-1]
    bf = lambda w: w.astype(BF16)
    vec = lambda g: g.reshape(1, -1)
    pp = p_prompt.reshape(depth, tp, ple)
    ps = p_sample.reshape(depth, ts, ple)

    x1_p, x1_s, hf_p, hf_s, pool_p, pool_s = _pool_front(
        x_prompt, x_sample, state_pool[0], vec(g_mix[0]), vec(g_ffn[0]), bf(w_pool[0]), vec(pool_scale[0]),
        ta=512, nbs=8)

    back = cache_k.shape[1]
    ck = cache_k.reshape(bs, back, n_kv * hd)
    cv = cache_v.reshape(bs, back, n_kv * hd)
    x3, k, v, q, kc_p, vc_p = _ffn0_post(
        hf_p, hf_s, x1_p, x1_s, pp, ps,
        bf(w1_dense[0]), bf(w3_dense[0]), bf(w2_dense[0]),
        vec(g_ple[0]), bf(w_ple_proj[0]), bf(w_ple_gate[0]),
        vec(g_kv), bf(w_k), vec(b_k), bf(w_v), vec(b_v),
        vec(g_mix[1]), bf(w_q[0]), vec(b_q[0]), tm=512, fc=512, sp=sp, back=back)

    o_p, o_s, kc_s, vc_s = _swa(q, k, v, ck, cv, sinks[0], bp=bp, sp=sp, bs=bs, ss=ss, tq=N_BACK * CHUNK, n_sub=8)

    wr = jnp.pad(w_router[0], ((0, 0), (0, ROUTE_LANES - n_exp)))
    br = jnp.pad(b_router[0], (0, ROUTE_LANES - n_exp), constant_values=NEG).reshape(1, ROUTE_LANES)
    x4, hm, route = _oproj_route(o_p, o_s, x3, bf(w_o[0]), vec(g_ffn[1]), wr, br, tm=512)

    y2 = _moe(hm, route, w1_moe[0], w3_moe[0], w2_moe[0], tm=512, fc=512)

    y_p, y_s = _final(x4, y2, route, pp, ps,
                      vec(g_ple[1]), bf(w_ple_proj[1]), bf(w_ple_gate[1]), vec(g_final), tm=512)

    heads = lambda c: c.reshape(c.shape[0], back, n_kv, hd)
    return (y_p.reshape(bp, sp, d), y_s.reshape(bs, ss, d), pool_p, pool_s,
            heads(kc_p), heads(vc_p), heads(kc_s), heads(vc_s))
```

```python
import functools

import jax
import jax.numpy as jnp
from jax import lax
from jax.experimental import pallas as pl
from jax.experimental.pallas import tpu as pltpu

EPS = 1e-6
NEG = -1e30
CHUNK = 64
N_BACK = 2
POOL_WINDOWS = (2, 4, 8, 16)
POOL_BUF = 15
HIST = 16
TOP_K = 2
ROUTE_LANES = 128
VMEM_LIMIT = 56 * 1024 * 1024

F32 = jnp.float32
BF16 = jnp.bfloat16


def _rms(x, g):
    return (x * lax.rsqrt(jnp.mean(x * x, axis=-1, keepdims=True) + EPS)) * g


def _dot(a, b):
    return jnp.dot(a, b, preferred_element_type=F32)


def _const_spec(shape):
    nd = len(shape)
    return pl.BlockSpec(shape, lambda *_: (0,) * nd, pipeline_mode=pl.Buffered(1))


def _params(n_axes):
    return pltpu.CompilerParams(dimension_semantics=("arbitrary",) * n_axes, vmem_limit_bytes=VMEM_LIMIT)


def _pool_front_kernel(*refs, prompt, nb, ta, d):
    (x_ref, hist_ref, gmix_ref, gffn_ref, wpool_ref, scale_ref,
     x1_ref, hf_ref, state_ref, e_ref, f2_ref, f4_ref, f8_ref) = refs
    gw = d // len(POOL_WINDOWS)
    j = pl.program_id(1)
    n = ta + HIST
    x = x_ref[...]
    g = gmix_ref[...]
    h = _rms(x, g)
    if prompt:
        hp = _rms(hist_ref[...], g)
        hp = jnp.where(j == 0, 0.0, hp)
        e_ref[:, 0:HIST, :] = hp
    else:
        e_ref[:, 0:1, :] = jnp.zeros((nb, 1, d), F32)
        e_ref[:, 1:HIST, :] = hist_ref[...]
    e_ref[:, HIST:n, :] = h
    zero_tail = jnp.zeros((nb, 8, d), F32)
    e_ref[:, n:n + 8, :] = zero_tail
    f2_ref[:, 0:n, :] = e_ref[:, 0:n, :] + e_ref[:, 1:n + 1, :]
    f2_ref[:, n:n + 8, :] = zero_tail
    f4_ref[:, 0:n, :] = f2_ref[:, 0:n, gw:] + f2_ref[:, 2:n + 2, gw:]
    f4_ref[:, n:n + 8, :] = zero_tail[:, :, gw:]
    f8_ref[:, 0:n, :] = f4_ref[:, 0:n, gw:] + f4_ref[:, 4:n + 4, gw:]
    f8_ref[:, n:n + 8, :] = zero_tail[:, :, 2 * gw:]
    sums = (
        f2_ref[:, HIST - 1:HIST - 1 + ta, 0:gw],
        f4_ref[:, HIST - 3:HIST - 3 + ta, 0:gw],
        f8_ref[:, HIST - 7:HIST - 7 + ta, 0:gw],
        f8_ref[:, HIST - 15:HIST - 15 + ta, gw:] + f8_ref[:, HIST - 7:HIST - 7 + ta, gw:],
    )
    t_idx = lax.broadcasted_iota(jnp.int32, (1, ta, 1), 1) + j * ta
    ys = []
    for gi, w in enumerate(POOL_WINDOWS):
        if prompt:
            cnt = jnp.minimum(t_idx + 1, w).astype(F32)
            mean = sums[gi] / cnt
        else:
            mean = sums[gi] * (1.0 / w)
        diff = mean - h[:, :, gi * gw:(gi + 1) * gw]
        ys.append(_dot(diff.reshape(nb * ta, gw).astype(BF16), wpool_ref[gi]))
    y = jnp.concatenate(ys, axis=-1) * scale_ref[...]
    x1 = x.reshape(nb * ta, d) + y
    x1_ref[...] = x1
    hf_ref[...] = _rms(x1, gffn_ref[...]).astype(BF16)
    if prompt:
        @pl.when(j == pl.num_programs(1) - 1)
        def _():
            state_ref[0] = e_ref[:, n - POOL_BUF:n, :]
    else:
        state_ref[0] = e_ref[:, n - POOL_BUF:n, :]


def _pool_front(x_p, x_s, state, gmix, gffn, wpool, scale, *, ta, nbs):
    bp, sp, d = x_p.shape
    bs, ss, _ = x_s.shape
    tp, ts = bp * sp, bs * ss
    gw = d // len(POOL_WINDOWS)
    assert sp % ta == 0 and ta % HIST == 0 and bs % nbs == 0
    vec = pl.BlockSpec((1, d), lambda *_: (0, 0))
    wspec = pl.BlockSpec((len(POOL_WINDOWS), gw, gw), lambda *_: (0, 0, 0))

    def scratch(nb, rows):
        return [pltpu.VMEM((nb, rows + HIST + 8, d), F32), pltpu.VMEM((nb, rows + HIST + 8, d), F32),
                pltpu.VMEM((nb, rows + HIST + 8, d - gw), F32), pltpu.VMEM((nb, rows + HIST + 8, d - 2 * gw), F32)]

    njp = sp // ta
    x1_p, hf_p, state_p = pl.pallas_call(
        functools.partial(_pool_front_kernel, prompt=True, nb=1, ta=ta, d=d),
        grid=(bp, njp),
        in_specs=[
            pl.BlockSpec((1, ta, d), lambda b, j: (b, j, 0)),
            pl.BlockSpec((1, HIST, d), lambda b, j: (b, jnp.maximum(j * (ta // HIST) - 1, 0), 0)),
            vec, vec, wspec, vec,
        ],
        out_specs=[
            pl.BlockSpec((ta, d), lambda b, j: (b * njp + j, 0)),
            pl.BlockSpec((ta, d), lambda b, j: (b * njp + j, 0)),
            pl.BlockSpec((1, 1, POOL_BUF, d), lambda b, j: (0, b, 0, 0)),
        ],
        out_shape=[jax.ShapeDtypeStruct((tp, d), F32), jax.ShapeDtypeStruct((tp, d), BF16),
                   jax.ShapeDtypeStruct((1, bp, POOL_BUF, d), F32)],
        scratch_shapes=scratch(1, ta),
        compiler_params=_params(2),
        name="pool_front_prompt",
    )(x_p, x_p, gmix, gffn, wpool, scale)

    rows = nbs * ss
    x1_s, hf_s, state_s = pl.pallas_call(
        functools.partial(_pool_front_kernel, prompt=False, nb=nbs, ta=ss, d=d),
        grid=(bs // nbs, 1),
        in_specs=[
            pl.BlockSpec((nbs, ss, d), lambda b, j: (b, 0, 0)),
            pl.BlockSpec((nbs, POOL_BUF, d), lambda b, j: (b, 0, 0)),
            vec, vec, wspec, vec,
        ],
        out_specs=[
            pl.BlockSpec((rows, d), lambda b, j: (b, 0)),
            pl.BlockSpec((rows, d), lambda b, j: (b, 0)),
            pl.BlockSpec((1, nbs, POOL_BUF, d), lambda b, j: (0, b, 0, 0)),
        ],
        out_shape=[jax.ShapeDtypeStruct((ts, d), F32), jax.ShapeDtypeStruct((ts, d), BF16),
                   jax.ShapeDtypeStruct((1, bs, POOL_BUF, d), F32)],
        scratch_shapes=scratch(nbs, ss),
        compiler_params=_params(2),
        name="pool_front_sample",
    )(x_s, state, gmix, gffn, wpool, scale)
    return x1_p, x1_s, hf_p, hf_s, state_p, state_s


def _swiglu_acc(x_bf, w1_ref, w3_ref, w2_ref, acc_ref, fc, before_chunk=None, after_chunk=None):
    d_ff = w1_ref.shape[-1]
    n_chunks = d_ff // fc
    for c in range(n_chunks):
        if before_chunk is not None:
            before_chunk(c, n_chunks)
        sl = slice(c * fc, (c + 1) * fc)
        h1 = _dot(x_bf, w1_ref[:, sl])
        h3 = _dot(x_bf, w3_ref[:, sl])
        a = (h1 * jax.nn.sigmoid(h1)) * h3
        acc_ref[...] += _dot(a.astype(BF16), w2_ref[sl, :])
        if after_chunk is not None:
            after_chunk(c, n_chunks)


def _ple_gate(x, p_bf, gple, wproj_ref, wgate_ref):
    e = _dot(p_bf, wproj_ref[...])
    gate = jax.nn.sigmoid(_dot(_rms(x, gple).astype(BF16), wgate_ref[...]))
    return x + gate * e


def _ffn0_post_kernel(hfp_ref, hfs_ref, x1p_ref, x1s_ref, pp_ref, ps_ref, w1_ref, w3_ref, w2_ref,
                      gple_ref, wproj_ref, wgate_ref, gkv_ref, wk_ref, bk_ref, wv_ref, bv_ref,
                      gmix_ref, wq_ref, bq_ref,
                      x3_ref, k_ref, v_ref, q_ref, kcp_ref, vcp_ref, acc_ref,
                      *, fc, n_prompt_tiles, tiles_per_seq):
    i = pl.program_id(0)
    is_prompt = i < n_prompt_tiles
    acc_ref[...] = jnp.where(is_prompt, x1p_ref[...], x1s_ref[...])
    _swiglu_acc(jnp.where(is_prompt, hfp_ref[...], hfs_ref[...]), w1_ref, w3_ref, w2_ref, acc_ref, fc)
    p = jnp.where(is_prompt, pp_ref[...], ps_ref[...]).astype(BF16)
    x3 = _ple_gate(acc_ref[...], p, gple_ref[...], wproj_ref, wgate_ref)
    x3_ref[...] = x3
    kv_in = _rms(x3, gkv_ref[...]).astype(BF16)
    k = _dot(kv_in, wk_ref[...]) + bk_ref[...]
    v = _dot(kv_in, wv_ref[...]) + bv_ref[...]
    k_ref[...] = k
    v_ref[...] = v
    hq = _rms(x3, gmix_ref[...]).astype(BF16)
    q_ref[...] = (_dot(hq, wq_ref[...]) + bq_ref[...]).astype(BF16)

    back = kcp_ref.shape[1]

    @pl.when(jnp.logical_and(is_prompt, i % tiles_per_seq == tiles_per_seq - 1))
    def _():
        kcp_ref[0] = k[k.shape[0] - back:]
        vcp_ref[0] = v[v.shape[0] - back:]


def _split_specs(tm, width, n_prompt_tiles, layer=None):
    if layer is None:
        return [pl.BlockSpec((tm, width), lambda i, *_: (jnp.minimum(i, n_prompt_tiles - 1), 0)),
                pl.BlockSpec((tm, width), lambda i, *_: (jnp.maximum(i - n_prompt_tiles, 0), 0))]
    return [pl.BlockSpec((None, tm, width), lambda i, *_: (layer, jnp.minimum(i, n_prompt_tiles - 1), 0)),
            pl.BlockSpec((None, tm, width), lambda i, *_: (layer, jnp.maximum(i - n_prompt_tiles, 0), 0))]


def _ffn0_post(hf_p, hf_s, x1_p, x1_s, pp, ps, w1, w3, w2, gple, wproj, wgate, gkv, wk, bk, wv, bv,
               gmix, wq, bq, *, tm, fc, sp, back):
    tp, d = x1_p.shape
    ts = x1_s.shape[0]
    t = tp + ts
    kvw = wk.shape[1]
    bp = tp // sp
    assert ts % tm == 0 and tp % tm == 0 and w1.shape[1] % fc == 0 and sp % tm == 0 and back <= tm
    npt = tp // tm
    row = lambda w: pl.BlockSpec((tm, w), lambda i: (i, 0))
    vec = lambda w: pl.BlockSpec((1, w), lambda i: (0, 0))
    pcache = pl.BlockSpec((1, back, kvw), lambda i: (jnp.minimum(i // (sp // tm), bp - 1), 0, 0))
    cache_shape = jax.ShapeDtypeStruct((bp, back, kvw), F32)
    return pl.pallas_call(
        functools.partial(_ffn0_post_kernel, fc=fc, n_prompt_tiles=npt, tiles_per_seq=sp // tm),
        grid=(t // tm,),
        in_specs=_split_specs(tm, d, npt) + _split_specs(tm, d, npt) + _split_specs(tm, pp.shape[2], npt, layer=0) + [
            _const_spec(w1.shape), _const_spec(w3.shape), _const_spec(w2.shape),
            vec(d), _const_spec(wproj.shape), _const_spec(wgate.shape),
            vec(d), _const_spec(wk.shape), vec(kvw), _const_spec(wv.shape), vec(kvw),
            vec(d), _const_spec(wq.shape), vec(d),
        ],
        out_specs=[row(d), row(kvw), row(kvw), row(d), pcache, pcache],
        out_shape=[jax.ShapeDtypeStruct((t, d), F32), jax.ShapeDtypeStruct((t, kvw), F32),
                   jax.ShapeDtypeStruct((t, kvw), F32), jax.ShapeDtypeStruct((t, d), BF16), cache_shape, cache_shape],
        scratch_shapes=[pltpu.VMEM((tm, d), F32)],
        compiler_params=_params(1),
        name="ffn0_post",
    )(hf_p, hf_s, x1_p, x1_s, pp, ps, w1, w3, w2, gple, wproj, wgate, gkv, wk, bk, wv, bv, gmix, wq, bq)


def _swa_kernel(*refs, prompt, tq, n_kv, group, hd):
    lanes = 128
    per = lanes // hd
    assert per == 2 and group % per == 0
    if prompt:
        n_sub = (len(refs) - 3) // 2 - 1
        sink_ref, q_ref, o_ref = refs[0], refs[1], refs[-1]
        kblk = [r[...] for r in refs[2:3 + n_sub]]
        vblk = [r[...] for r in refs[3 + n_sub:4 + 2 * n_sub]]
        tiles = [(s * tq, jnp.concatenate(kblk[s:s + 2], axis=0), jnp.concatenate(vblk[s:s + 2], axis=0),
                  N_BACK - (n_sub * pl.program_id(1) + s) * (tq // CHUNK)) for s in range(n_sub)]
    else:
        sink_ref, q_ref, ck_ref, kn_ref, cv_ref, vn_ref, o_ref, kc_ref, vc_ref = refs
        pad = jnp.zeros((tq, kn_ref.shape[1]), F32)
        back = ck_ref.shape[1]
        tiles = []
        for s in range(ck_ref.shape[0]):
            kk = jnp.concatenate([ck_ref[s], kn_ref[s * tq:(s + 1) * tq], pad], axis=0)
            vv = jnp.concatenate([cv_ref[s], vn_ref[s * tq:(s + 1) * tq], pad], axis=0)
            kc_ref[s] = kk[tq:tq + back]
            vc_ref[s] = vv[tq:tq + back]
            tiles.append((s * tq, kk, vv, 0))
    for r0, kk, vv, first_chunk in tiles:
        _swa_tile(sink_ref, q_ref, o_ref, r0, kk, vv, first_chunk, tq=tq, n_kv=n_kv, group=group, hd=hd)


def _swa_tile(sink_ref, q_ref, o_ref, r0, kk, vv, first_chunk, *, tq, n_kv, group, hd):
    nk = kk.shape[0]
    lanes = 128
    per = lanes // hd
    assert nk % lanes == 0
    rows = slice(r0, r0 + tq)
    qc = lax.broadcasted_iota(jnp.int32, (tq, per * nk), 0) // CHUNK
    kc = (lax.broadcasted_iota(jnp.int32, (tq, per * nk), 1) % nk) // CHUNK
    bias = jnp.where(kc >= jnp.maximum(qc, first_chunk), jnp.where(kc <= qc + N_BACK, 0.0, NEG), NEG)
    lane = lax.broadcasted_iota(jnp.int32, (nk, lanes), 1)
    ones_a = jnp.where(lane < hd, 1.0, 0.0)
    ones_b = jnp.where(lane >= hd, 1.0, 0.0)
    out_lane = lax.broadcasted_iota(jnp.int32, (tq, lanes), 1)
    for kvh in range(n_kv):
        tile, half = divmod(kvh, per)
        keep = (lane < hd) if half == 0 else (lane >= hd)

        def both_halves(x):
            own = jnp.where(keep, x[:, tile * lanes:(tile + 1) * lanes], 0.0)
            other = pltpu.roll(own, hd, axis=1)
            return (own, other) if half == 0 else (other, own)

        k_a, k_b = both_halves(kk * (hd ** -0.5))
        v_a, v_b = both_halves(vv)
        kbd = jnp.concatenate([k_a, k_b], axis=0).astype(BF16)
        vobd = jnp.concatenate([jnp.concatenate([v_a, ones_a], axis=1),
                                jnp.concatenate([v_b, ones_b], axis=1)], axis=0).astype(BF16)
        for pr in range(group // per):
            head_a = kvh * group + pr * per
            col0 = head_a * hd
            s = lax.dot_general(q_ref[rows, col0:col0 + lanes], kbd, (((1,), (1,)), ((), ())),
                                preferred_element_type=F32) + bias
            sink_a, sink_b = sink_ref[head_a], sink_ref[head_a + 1]
            m_a = jnp.maximum(jnp.max(s[:, :nk], axis=-1, keepdims=True), sink_a)
            m_b = jnp.maximum(jnp.max(s[:, nk:], axis=-1, keepdims=True), sink_b)
            p = jnp.concatenate([jnp.exp(s[:, :nk] - m_a), jnp.exp(s[:, nk:] - m_b)], axis=1).astype(BF16)
            r = _dot(p, vobd)
            sink_term = jnp.where(out_lane < hd, jnp.exp(sink_a - m_a), jnp.exp(sink_b - m_b))
            o_ref[rows, col0:col0 + lanes] = (r[:, :lanes] / (r[:, lanes:] + sink_term)).astype(BF16)


def _swa(q, k, v, cache_k, cache_v, sinks, *, bp, sp, bs, ss, tq, n_sub):
    t, d = q.shape
    kvw = k.shape[1]
    tp = bp * sp
    n_heads = sinks.shape[0]
    hd = d // n_heads
    n_kv = kvw // hd
    back = N_BACK * CHUNK
    assert tq == back and sp % tq == 0 and ss == CHUNK and cache_k.shape[1] == back
    smem = pl.BlockSpec(memory_space=pltpu.SMEM)
    nj = sp // tq
    kw = dict(n_kv=n_kv, group=n_heads // n_kv, hd=hd)
    assert nj % n_sub == 0
    njs = nj // n_sub
    kv_blk = lambda s: pl.BlockSpec((tq, kvw), lambda b, j: (b * nj + jnp.maximum(n_sub * j + s, 0), 0))
    kv_blks = [kv_blk(s) for s in range(-1, n_sub)]
    o_p = pl.pallas_call(
        functools.partial(_swa_kernel, prompt=True, tq=tq, **kw),
        grid=(bp, njs),
        in_specs=[smem, pl.BlockSpec((n_sub * tq, d), lambda b, j: (b * njs + j, 0))] + kv_blks + kv_blks,
        out_specs=pl.BlockSpec((n_sub * tq, d), lambda b, j: (b * njs + j, 0)),
        out_shape=jax.ShapeDtypeStruct((tp, d), BF16),
        compiler_params=_params(2),
        name="swa_prompt",
    )(sinks, q, *([k] * (n_sub + 1)), *([v] * (n_sub + 1)))
    rows = n_sub * ss
    assert bs % n_sub == 0 and tp % rows == 0
    off = tp // rows
    cache_blk = pl.BlockSpec((n_sub, back, kvw), lambda b: (b, 0, 0))
    new_blk = pl.BlockSpec((rows, kvw), lambda b: (off + b, 0))
    o_s, kc_s, vc_s = pl.pallas_call(
        functools.partial(_swa_kernel, prompt=False, tq=ss, **kw),
        grid=(bs // n_sub,),
        in_specs=[smem, pl.BlockSpec((rows, d), lambda b: (off + b, 0)), cache_blk, new_blk, cache_blk, new_blk],
        out_specs=[pl.BlockSpec((rows, d), lambda b: (b, 0)), cache_blk, cache_blk],
        out_shape=[jax.ShapeDtypeStruct((bs * ss, d), BF16), jax.ShapeDtypeStruct((bs, back, kvw), F32),
                   jax.ShapeDtypeStruct((bs, back, kvw), F32)],
        compiler_params=_params(1),
        name="swa_sample",
    )(sinks, q, cache_k, k, cache_v, v)
    return o_p, o_s, kc_s, vc_s


def _oproj_route_kernel(op_ref, os_ref, x3_ref, wo_ref, gffn_ref, wr_ref, br_ref, x4_ref, hm_ref, route_ref,
                        *, n_prompt_tiles):
    o = jnp.where(pl.program_id(0) < n_prompt_tiles, op_ref[...], os_ref[...])
    x4 = x3_ref[...] + _dot(o, wo_ref[...])
    x4_ref[...] = x4
    hm = _rms(x4, gffn_ref[...])
    hm_ref[...] = hm
    hm_hi = hm.astype(BF16)
    hm_lo = (hm - hm_hi.astype(F32)).astype(BF16)
    wr = wr_ref[...]
    wr_hi = wr.astype(BF16)
    wr_lo = (wr - wr_hi.astype(F32)).astype(BF16)
    hi_both = _dot(hm_hi, jnp.concatenate([wr_hi, wr_lo], axis=1))
    logits = (hi_both[:, :ROUTE_LANES] + (_dot(hm_lo, wr_hi) + hi_both[:, ROUTE_LANES:])) + br_ref[...]
    lane = lax.broadcasted_iota(jnp.int32, logits.shape, 1).astype(F32)
    m1 = jnp.max(logits, axis=-1, keepdims=True)
    i1 = jnp.min(jnp.where(logits == m1, lane, float(ROUTE_LANES)), axis=-1, keepdims=True)
    rest = jnp.where(lane == i1, -jnp.inf, logits)
    m2 = jnp.max(rest, axis=-1, keepdims=True)
    i2 = jnp.min(jnp.where(rest == m2, lane, float(ROUTE_LANES)), axis=-1, keepdims=True)
    e2 = jnp.exp(m2 - m1)
    g1 = 1.0 / (1.0 + e2)
    g2 = e2 / (1.0 + e2)
    route = jnp.where(lane == 0.0, i1, jnp.where(lane == 1.0, i2,
                      jnp.where(lane == 2.0, g1, jnp.where(lane == 3.0, g2, 0.0))))
    route_ref[...] = route


def _oproj_route(o_p, o_s, x3, wo, gffn, wr, br, *, tm):
    t, d = x3.shape
    tp = o_p.shape[0]
    assert tp % tm == 0 and o_s.shape[0] % tm == 0
    row = lambda w: pl.BlockSpec((tm, w), lambda i: (i, 0))
    return pl.pallas_call(
        functools.partial(_oproj_route_kernel, n_prompt_tiles=tp // tm),
        grid=(t // tm,),
        in_specs=_split_specs(tm, d, tp // tm) + [row(d), _const_spec(wo.shape), pl.BlockSpec((1, d), lambda i: (0, 0)),
                  _const_spec(wr.shape), pl.BlockSpec((1, ROUTE_LANES), lambda i: (0, 0))],
        out_specs=[row(d), row(d), row(ROUTE_LANES)],
        out_shape=[jax.ShapeDtypeStruct((t, d), F32), jax.ShapeDtypeStruct((t, d), F32),
                   jax.ShapeDtypeStruct((t, ROUTE_LANES), F32)],
        compiler_params=_params(1),
        name="oproj_route",
    )(o_p, o_s, x3, wo, gffn, wr, br)


def _moe_kernel(texp_ref, first_ref, nreal_ref, rtok_ref, rdst_ref, hm_ref, w1_hbm, w3_hbm, w2_hbm, y_ref,
                xbuf, obuf, acc_ref, w1_ref, w3_ref, w2_ref, st13, st2, gsem, ssem, wsem,
                *, tm, fc):
    i = pl.program_id(0)
    n_slots = st2.shape[0]

    def gather_row(base, r, xb, sem):
        tok = rtok_ref[base + r]
        pltpu.make_async_copy(hm_ref.at[pl.ds(tok, 1), :], xb.at[pl.ds(r, 1), :], sem).start()

    def scatter_row(base, r, ob, sem):
        dst = rdst_ref[base + r]
        pltpu.make_async_copy(ob.at[pl.ds(r, 1), :], y_ref.at[pl.ds(dst, 1), :], sem).start()

    def weight_copies(c, step=None):
        e = texp_ref[i if step is None else step]
        slot = c % n_slots
        cols = pl.ds(c * fc, fc)
        return (pltpu.make_async_copy(w1_hbm.at[e, :, cols], st13.at[slot, 0], wsem.at[slot]),
                pltpu.make_async_copy(w3_hbm.at[e, :, cols], st13.at[slot, 1], wsem.at[slot]),
                pltpu.make_async_copy(w2_hbm.at[e, cols, :], st2.at[slot], wsem.at[slot]))

    def start_weights(c, step=None):
        for cp in weight_copies(c, step):
            cp.start(1)

    def stage_weights(c, n_chunks):
        for cp in weight_copies(c):
            cp.wait()
        slot = c % n_slots
        w1_ref[:, c * fc:(c + 1) * fc] = st13[slot, 0].astype(BF16)
        w3_ref[:, c * fc:(c + 1) * fc] = st13[slot, 1].astype(BF16)
        w2_ref[c * fc:(c + 1) * fc, :] = st2[slot].astype(BF16)
        if c + n_slots < n_chunks:
            start_weights(c + n_slots)

    def wait_gather(xb, sem):
        pltpu.make_async_copy(hm_ref.at[pl.ds(0, tm), :], xb, sem).wait()

    def wait_scatter(ob, sem):
        pltpu.make_async_copy(ob, y_ref.at[pl.ds(0, tm), :], sem).wait()

    @pl.when(i == 0)
    def _():
        obuf[...] = jnp.zeros(obuf.shape, F32)
        lax.fori_loop(0, tm, lambda r, c: (gather_row(0, r, xbuf.at[0], gsem.at[0]), c)[1], 0)
        n_out = y_ref.shape[0] - 2 * tm
        fills = [pltpu.make_async_copy(obuf.at[s], y_ref.at[pl.ds(n_out + s * tm, tm), :], ssem.at[s])
                 for s in range(2)]
        for cp in fills:
            cp.start()
        for cp in fills:
            cp.wait()

    is_first = first_ref[i] == 1
    prev_first = jnp.logical_or(i == 0, first_ref[jnp.maximum(i - 1, 0)] == 1)

    @pl.when(jnp.logical_and(is_first, prev_first))
    def _():
        for c in range(n_slots):
            start_weights(c)

    @pl.when(jnp.logical_and(first_ref[i + 1] == 1, jnp.logical_not(is_first)))
    def _():
        for c in range(n_slots):
            start_weights(c, i + 1)

    p = i % 2
    xb, ob, xb_next, ob_prev = xbuf.at[p], obuf.at[p], xbuf.at[1 - p], obuf.at[1 - p]

    def step(streaming):
        wait_gather(xb, gsem.at[p])
        acc_ref[...] = jnp.zeros(acc_ref.shape, F32)
        g_base = (i + 1) * tm
        s_base = i * tm

        def move_rows(c, n_chunks):
            per = -(-2 * tm // n_chunks)
            for m in range(c * per, min((c + 1) * per, 2 * tm)):
                if m < tm:
                    gather_row(g_base, m, xb_next, gsem.at[1 - p])
                else:
                    scatter_row(s_base, m - tm, ob_prev, ssem.at[1 - p])

        _swiglu_acc(xb[...].astype(BF16), w1_ref, w3_ref, w2_ref, acc_ref, fc,
                    before_chunk=stage_weights if streaming else None, after_chunk=move_rows)

        @pl.when(i >= 1)
        def _():
            wait_scatter(ob, ssem.at[p])
        ob[...] = acc_ref[...]

    n_real = nreal_ref[0]

    @pl.when(is_first)
    def _():
        step(True)

    @pl.when(jnp.logical_and(i < n_real, jnp.logical_not(is_first)))
    def _():
        step(False)

    @pl.when(i == n_real)
    def _():
        wait_gather(xb, gsem.at[p])

        @pl.when(i >= 1)
        def _():
            wait_scatter(ob, ssem.at[p])
        lax.fori_loop(0, tm, lambda r, c: (scatter_row(i * tm, r, ob_prev, ssem.at[1 - p]), c)[1], 0)
        wait_scatter(ob_prev, ssem.at[1 - p])


def _moe(hm, route, w1, w3, w2, *, tm, fc):
    t, d = hm.shape
    n_exp = w1.shape[0]
    n_assign = t * TOP_K
    nt = n_assign // tm + n_exp
    assert n_assign % tm == 0
    e_flat = route[:, :TOP_K].astype(jnp.int32).reshape(-1)
    order = jnp.argsort(e_flat, stable=True).astype(jnp.int32)
    counts = jnp.sum((e_flat[:, None] == jnp.arange(n_exp, dtype=jnp.int32)[None, :]).astype(jnp.int32), axis=0)
    tiles_e = (counts + tm - 1) // tm
    tile_end = jnp.cumsum(tiles_e)
    tile_start = tile_end - tiles_e
    cnt_start = jnp.cumsum(counts) - counts
    tile_ids = jnp.arange(nt, dtype=jnp.int32)
    texp = jnp.minimum(jnp.sum((tile_ids[:, None] >= tile_end[None, :]).astype(jnp.int32), axis=1), n_exp - 1)
    within = (tile_ids - tile_start[texp]) * tm
    nval = jnp.where(tile_ids < tile_end[-1], jnp.clip(counts[texp] - within, 0, tm), 0).astype(jnp.int32)
    r_in = jnp.arange(tm, dtype=jnp.int32)[None, :]
    src = jnp.clip(cnt_start[texp][:, None] + within[:, None] + r_in, 0, n_assign - 1)
    real = r_in < nval[:, None]
    assign = jnp.where(real, order[src], 0)
    scratch_row = TOP_K * t + (tile_ids[:, None] % 2) * tm + r_in
    rtok = jnp.concatenate([(assign // TOP_K).reshape(-1), jnp.zeros((2 * tm,), jnp.int32)])
    rdst = jnp.concatenate([TOP_K * t + tm + r_in[0],
                            jnp.where(real, (assign % TOP_K) * t + assign // TOP_K, scratch_row).reshape(-1)])
    texp = jnp.concatenate([texp, texp[-1:], texp[-1:]])
    prev_exp = jnp.concatenate([jnp.full((1,), -1, jnp.int32), texp[:-1]])
    first = jnp.where(jnp.arange(nt + 2) < tile_end[-1], texp != prev_exp, False).astype(jnp.int32)
    n_steps = nt + 1
    d_ff = w1.shape[2]
    n_slots = 3
    assert d_ff % fc == 0 and d_ff // fc >= n_slots

    hbm = pl.BlockSpec(memory_space=pl.ANY)
    y2 = pl.pallas_call(
        functools.partial(_moe_kernel, tm=tm, fc=fc),
        grid_spec=pltpu.PrefetchScalarGridSpec(
            num_scalar_prefetch=5,
            grid=(n_steps,),
            in_specs=[hbm, hbm, hbm, hbm],
            out_specs=hbm,
            scratch_shapes=[pltpu.VMEM((2, tm, d), F32), pltpu.VMEM((2, tm, d), F32), pltpu.VMEM((tm, d), F32),
                            pltpu.VMEM((d, d_ff), BF16), pltpu.VMEM((d, d_ff), BF16), pltpu.VMEM((d_ff, d), BF16),
               pltpu.VMEM((n_slots, 2, d, fc), F32), pltpu.VMEM((n_slots, fc, d), F32),
               pltpu.SemaphoreType.DMA((2,)), pltpu.SemaphoreType.DMA((2,)), pltpu.SemaphoreType.DMA((n_slots,))],
        ),
        out_shape=jax.ShapeDtypeStruct((TOP_K * t + 2 * tm, d), F32),
        compiler_params=_params(1),
        name="moe",
    )(texp, first, tile_end[-1:].astype(jnp.int32), rtok, rdst, hm, w1, w3, w2)
    return y2


def _final_kernel(x4_ref, ya_ref, yb_ref, route_ref, pp_ref, ps_ref, gple_ref, wproj_ref, wgate_ref, gfin_ref,
                  yp_ref, ys_ref, *, n_prompt_tiles):
    i = pl.program_id(0)
    route = route_ref[...]
    g1 = route[:, 2:3]
    g2 = route[:, 3:4]
    x5 = x4_ref[...] + (g1 * ya_ref[...] + g2 * yb_ref[...])
    p = jnp.where(i < n_prompt_tiles, pp_ref[...], ps_ref[...]).astype(BF16)
    x6 = _ple_gate(x5, p, gple_ref[...], wproj_ref, wgate_ref)
    y = _rms(x6, gfin_ref[...])

    @pl.when(i < n_prompt_tiles)
    def _():
        yp_ref[...] = y

    @pl.when(i >= n_prompt_tiles)
    def _():
        ys_ref[...] = y


def _final(x4, y2, route, pp, ps, gple, wproj, wgate, gfin, *, tm):
    t, d = x4.shape
    tp, ts = pp.shape[1], ps.shape[1]
    npt = tp // tm
    assert tp % tm == 0 and ts % tm == 0
    row = lambda w: pl.BlockSpec((tm, w), lambda i: (i, 0))
    vec = pl.BlockSpec((1, d), lambda i: (0, 0))
    return pl.pallas_call(
        functools.partial(_final_kernel, n_prompt_tiles=npt),
        grid=(t // tm,),
        in_specs=[row(d), row(d), pl.BlockSpec((tm, d), lambda i: (t // tm + i, 0)), row(ROUTE_LANES)]
        + _split_specs(tm, pp.shape[2], npt, layer=1) + [vec, _const_spec(wproj.shape), _const_spec(wgate.shape), vec],
        out_specs=_split_specs(tm, d, npt),
        out_shape=[jax.ShapeDtypeStruct((tp, d), F32), jax.ShapeDtypeStruct((ts, d), F32)],
        compiler_params=_params(1),
        name="final",
    )(x4, y2, y2, route, pp, ps, gple, wproj, wgate, gfin)


def kernel(x_prompt, x_sample, state_pool, cache_k, cache_v, p_prompt, p_sample, g_mix, g_ffn, g_ple, g_kv, g_final,
           w_pool, pool_scale, w_q, b_q, w_k, b_k, w_v, b_v, w_o, sinks, w1_dense, w3_dense, w2_dense, w_router,
           b_router, w1_moe, w3_moe, w2_moe, w_ple_proj, w_ple_gate):
    bp, sp, d = x_prompt.shape
    bs, ss, _ = x_sample.shape
    depth = g_mix.shape[0]
    assert depth == 2 and w_pool.shape[0] == 1 and w_q.shape[0] == 1 and w1_dense.shape[0] == 1 and w1_moe.shape[0] == 1
    tp, ts = bp * sp, bs * ss
    n_kv, hd = cache_k.shape[2], cache_k.shape[3]
    n_exp = w_router.shape[2]
    ple = p_prompt.shape[-1]
    bf = lambda w: w.astype(BF16)
    vec = lambda g: g.reshape(1, -1)
    pp = p_prompt.reshape(depth, tp, ple)
    ps = p_sample.reshape(depth, ts, ple)

    x1_p, x1_s, hf_p, hf_s, pool_p, pool_s = _pool_front(
        x_prompt, x_sample, state_pool[0], vec(g_mix[0]), vec(g_ffn[0]), bf(w_pool[0]), vec(pool_scale[0]),
        ta=512, nbs=8)

    back = cache_k.shape[1]
    ck = cache_k.reshape(bs, back, n_kv * hd)
    cv = cache_v.reshape(bs, back, n_kv * hd)
    x3, k, v, q, kc_p, vc_p = _ffn0_post(
        hf_p, hf_s, x1_p, x1_s, pp, ps,
        bf(w1_dense[0]), bf(w3_dense[0]), bf(w2_dense[0]),
        vec(g_ple[0]), bf(w_ple_proj[0]), bf(w_ple_gate[0]),
        vec(g_kv), bf(w_k), vec(b_k), bf(w_v), vec(b_v),
        vec(g_mix[1]), bf(w_q[0]), vec(b_q[0]), tm=512, fc=512, sp=sp, back=back)

    o_p, o_s, kc_s, vc_s = _swa(q, k, v, ck, cv, sinks[0], bp=bp, sp=sp, bs=bs, ss=ss, tq=N_BACK * CHUNK, n_sub=8)

    wr = jnp.pad(w_router[0], ((0, 0), (0, ROUTE_LANES - n_exp)))
    br = jnp.pad(b_router[0], (0, ROUTE_LANES - n_exp), constant_values=NEG).reshape(1, ROUTE_LANES)
    x4, hm, route = _oproj_route(o_p, o_s, x3, bf(w_o[0]), vec(g_ffn[1]), wr, br, tm=1024)

    y2 = _moe(hm, route, w1_moe[0], w3_moe[0], w2_moe[0], tm=512, fc=512)

    y_p, y_s = _final(x4, y2, route, pp, ps,
                      vec(g_ple[1]), bf(w_ple_proj[1]), bf(w_ple_gate[1]), vec(g_final), tm=512)

    heads = lambda c: c.reshape(c.shape[0], back, n_kv, hd)
    return (y_p.reshape(bp, sp, d), y_s.reshape(bs, ss, d), pool_p, pool_s,
            heads(kc_p), heads(vc_p), heads(kc_s), heads(vc_s))
```
